```python
import jax, jax.numpy as jnp
from jax import lax
import numpy as np

D_MODEL = 4096
BATCH = 4
SEQ = 2048
DEPTH = 1
DEC_BATCH = 128
DEC_SEQ = 8
PAST_LEN = 16384
PAGE_SIZE = 128

MLA_HEADS = 16
QK_NOPE = 128
ROPE_DIM = 64
V_DIM = 128
Q_LORA = 896
KV_LORA = 512
ROPE_THETA = 10000.0
Q_BLOCK = 128
ATTN_SCALE = (QK_NOPE + ROPE_DIM) ** -0.5
RWKV_HEAD = 64
RWKV_WIDTH = D_MODEL - MLA_HEADS * V_DIM
RWKV_HEADS = RWKV_WIDTH // RWKV_HEAD
DECAY_LORA = 128
ICLR_LORA = 128
GATE_LORA = 480
RWKV_PROJ = 3 * RWKV_WIDTH + DECAY_LORA + ICLR_LORA + GATE_LORA
RWKV_SPLITS = (RWKV_WIDTH, 2 * RWKV_WIDTH, 3 * RWKV_WIDTH,
               3 * RWKV_WIDTH + DECAY_LORA, 3 * RWKV_WIDTH + DECAY_LORA + ICLR_LORA)
MLA_PROJ = Q_LORA + KV_LORA + ROPE_DIM
IN_WIDTH = MLA_PROJ + RWKV_PROJ
D_FF = 11008
CONV_W = 3
NORM_EPS = 1e-6
GN_EPS = 64e-5

kernel_name = "hymba_mla_rwkv7_convffn_step"


def rms_norm(x, g):
    xf = x.astype(jnp.float32)
    y = xf * lax.rsqrt(jnp.mean(xf * xf, axis=-1, keepdims=True) + NORM_EPS)
    return (y * g.astype(jnp.float32)).astype(x.dtype)


def rope_tables(pos):
    half = ROPE_DIM // 2
    inv = ROPE_THETA ** (-jnp.arange(half, dtype=jnp.float32) * 2.0 / ROPE_DIM)
    ang = pos.astype(jnp.float32)[:, None] * inv[None, :]
    return jnp.cos(ang), jnp.sin(ang)


def apply_rope(x, cos, sin):
    x1, x2 = jnp.split(x.astype(jnp.float32), 2, axis=-1)
    return jnp.concatenate([x1 * cos - x2 * sin, x1 * sin + x2 * cos], axis=-1).astype(x.dtype)


def setup_inputs(seed: int = 0) -> dict:
    key = jax.random.key(seed)
    ks = iter(jax.random.split(key, 48))

    def nrm(shape, s=1.0):
        return jax.random.normal(next(ks), shape, jnp.float32) * s

    def uni(shape, lo, hi):
        return jax.random.uniform(next(ks), shape, jnp.float32, lo, hi)

    n_pages = PAST_LEN // PAGE_SIZE
    n_phys = (5 * DEC_BATCH * n_pages) // 4
    perm = jax.random.permutation(next(ks), n_phys)[: DEC_BATCH * n_pages]
    page_table = perm.reshape(DEC_BATCH, n_pages).astype(jnp.int32)
    d = D_MODEL
    return {
        "x_prompt": nrm((BATCH, SEQ, d)),
        "x_sample": nrm((DEC_BATCH, DEC_SEQ, d)),
        "c_prompt": nrm((BATCH, d)),
        "c_sample": nrm((DEC_BATCH, d)),
        "cache_kv_latent": nrm((n_phys, PAGE_SIZE, KV_LORA)),
        "cache_k_rope": nrm((n_phys, PAGE_SIZE, ROPE_DIM)),
        "page_table": page_table,
        "state_rwkv_shift": nrm((DEC_BATCH, RWKV_PROJ)),
        "state_rwkv_wkv": nrm((DEC_BATCH, RWKV_HEADS, RWKV_HEAD, RWKV_HEAD), 0.5),
        "state_ffn_conv": nrm((DEC_BATCH, CONV_W - 1, D_FF)),
        "w_ada": nrm((d, 6 * d), 0.5 * d ** -0.5),
        "b_ada": nrm((6 * d,), 0.01),
        "g_pre_mix": 1.0 + nrm((d,), 0.01),
        "g_post_mix": 1.0 + nrm((d,), 0.01),
        "g_pre_ffn": 1.0 + nrm((d,), 0.01),
        "g_post_ffn": 1.0 + nrm((d,), 0.01),
        "w_in": nrm((d, IN_WIDTH), d ** -0.5),
        "g_q_latent": 1.0 + nrm((Q_LORA,), 0.01),
        "w_uq": nrm((Q_LORA, MLA_HEADS * (QK_NOPE + ROPE_DIM)), Q_LORA ** -0.5),
        "g_kv_latent": 1.0 + nrm((KV_LORA,), 0.01),
        "w_uk": nrm((KV_LORA, MLA_HEADS, QK_NOPE), KV_LORA ** -0.5),
        "w_uv": nrm((KV_LORA, MLA_HEADS, V_DIM), KV_LORA ** -0.5),
        "mu_shift": uni((RWKV_PROJ,), 0.0, 1.0),
        "w0": uni((RWKV_WIDTH,), -3.0, 1.0),
        "w_decay_up": nrm((DECAY_LORA, RWKV_WIDTH), 0.5 * DECAY_LORA ** -0.5),
        "a0": nrm((RWKV_WIDTH,), 0.1),
        "w_iclr_up": nrm((ICLR_LORA, RWKV_WIDTH), 0.5 * ICLR_LORA ** -0.5),
        "w_gate_up": nrm((GATE_LORA, RWKV_WIDTH), GATE_LORA ** -0.5),
        "k_k": 0.85 + nrm((RWKV_WIDTH,), 0.02),
        "k_a": 1.0 + nrm((RWKV_WIDTH,), 0.02),
        "r_k": nrm((RWKV_HEADS, RWKV_HEAD), 0.1),
        "ln_x_w": 1.0 + nrm((RWKV_WIDTH,), 0.01),
        "ln_x_b": nrm((RWKV_WIDTH,), 0.01),
        "w_out": nrm((d, d), d ** -0.5),
        "w_ffn_in": nrm((d, 2 * D_FF), d ** -0.5),
        "conv_w": nrm((CONV_W, D_FF), CONV_W ** -0.5),
        "conv_b": nrm((D_FF,), 0.01),
        "w_ffn_out": nrm((D_FF, d), D_FF ** -0.5),
    }


def reference(x_prompt, x_sample, c_prompt, c_sample,
              cache_kv_latent, cache_k_rope, page_table,
              state_rwkv_shift, state_rwkv_wkv, state_ffn_conv,
              w_ada, b_ada, g_pre_mix, g_post_mix, g_pre_ffn, g_post_ffn,
              w_in, g_q_latent, w_uq, g_kv_latent, w_uk, w_uv,
              mu_shift, w0, w_decay_up, a0, w_iclr_up, w_gate_up,
              k_k, k_a, r_k, ln_x_w, ln_x_b, w_out,
              w_ffn_in, conv_w, conv_b, w_ffn_out):
    f32 = jnp.float32

    def attend_prompt(q_abs, q_pe, kv_c, k_pe):
        B, T, H, C = q_abs.shape
        nb = T // Q_BLOCK
        qa = jnp.swapaxes(q_abs.reshape(B, nb, Q_BLOCK, H, C), 0, 1)
        qp = jnp.swapaxes(q_pe.reshape(B, nb, Q_BLOCK, H, ROPE_DIM), 0, 1)
        kpos = jnp.arange(T)

        def block(args):
            i, qa_b, qp_b = args
            s = (jnp.einsum('bqhc,bkc->bhqk', qa_b, kv_c)
                 + jnp.einsum('bqhr,bkr->bhqk', qp_b, k_pe)).astype(f32) * ATTN_SCALE
            qpos = i * Q_BLOCK + jnp.arange(Q_BLOCK)
            s = jnp.where(kpos[None, :] <= qpos[:, None], s, -jnp.inf)
            p = jax.nn.softmax(s, axis=-1).astype(kv_c.dtype)
            return jnp.einsum('bhqk,bkc->bqhc', p, kv_c)

        o = lax.map(block, (jnp.arange(nb), qa, qp))
        return jnp.swapaxes(o, 0, 1).reshape(B, T, H, C)

    def attend_sample(q_abs, q_pe, kv_c, k_pe):
        T = q_abs.shape[1]
        causal = jnp.tril(jnp.ones((T, T), dtype=bool))

        def one_seq(args):
            pages, qa, qp, kc_new, kp_new = args
            kc_past = cache_kv_latent[pages].reshape(-1, KV_LORA)
            kp_past = cache_k_rope[pages].reshape(-1, ROPE_DIM)
            s_past = (jnp.einsum('qhc,kc->hqk', qa, kc_past)
                      + jnp.einsum('qhr,kr->hqk', qp, kp_past)).astype(f32) * ATTN_SCALE
            s_new = (jnp.einsum('qhc,kc->hqk', qa, kc_new)
                     + jnp.einsum('qhr,kr->hqk', qp, kp_new)).astype(f32) * ATTN_SCALE
            s_new = jnp.where(causal[None], s_new, -jnp.inf)
            p = jax.nn.softmax(jnp.concatenate([s_past, s_new], axis=-1), axis=-1)
            n_past = kc_past.shape[0]
            o = (jnp.einsum('hqk,kc->qhc', p[..., :n_past].astype(kc_past.dtype), kc_past)
                 + jnp.einsum('hqk,kc->qhc', p[..., n_past:].astype(kc_new.dtype), kc_new))
            return o

        return lax.map(one_seq, (page_table, q_abs, q_pe, kv_c, k_pe))

    def rwkv7(rw, shift_prev, wkv_prev):
        B, T, _ = rw.shape
        prev = jnp.concatenate([shift_prev[:, None, :].astype(rw.dtype), rw[:, :-1]], axis=1)
        mixed = rw + (prev - rw) * mu_shift
        r, k, v, wd, ad, gd = jnp.split(mixed, RWKV_SPLITS, axis=-1)
        w_log = -jax.nn.softplus(-(w0 + jnp.tanh(wd) @ w_decay_up)) - 0.5
        decay = jnp.exp(-jnp.exp(w_log.astype(f32)))
        a = jax.nn.sigmoid(a0 + ad @ w_iclr_up)
        g = jax.nn.sigmoid(gd) @ w_gate_up

        def heads(t):
            return t.reshape(B, T, RWKV_HEADS, RWKV_HEAD).astype(f32)

        kk = heads(k * k_k)
        kk = kk / jnp.maximum(jnp.sqrt(jnp.sum(kk * kk, axis=-1, keepdims=True)), 1e-12)
        k_h = heads(k * (1.0 + (a - 1.0) * k_a))
        r_h, v_h, a_h, w_h = heads(r), heads(v), heads(a), heads(decay)

        def step(S, inp):
            r_t, w_t, k_t, v_t, kk_t, a_t = inp
            sa = jnp.einsum('bhvk,bhk->bhv', S, -kk_t)
            S = (S * w_t[:, :, None, :]
                 + jnp.einsum('bhv,bhk->bhvk', sa, kk_t * a_t)
                 + jnp.einsum('bhv,bhk->bhvk', v_t, k_t))
            return S, jnp.einsum('bhvk,bhk->bhv', S, r_t)

        tm = lambda t: jnp.swapaxes(t, 0, 1)
        S_fin, y = lax.scan(step, wkv_prev.astype(f32),
                            (tm(r_h), tm(w_h), tm(k_h), tm(v_h), tm(kk), tm(a_h)))
        y = tm(y)
        mu = jnp.mean(y, axis=-1, keepdims=True)
        var = jnp.mean(jnp.square(y - mu), axis=-1, keepdims=True)
        yn = ((y - mu) * lax.rsqrt(var + GN_EPS)).reshape(B, T, RWKV_WIDTH) * ln_x_w + ln_x_b
        bonus = (jnp.sum(r_h * k_h * r_k, axis=-1, keepdims=True) * v_h).reshape(B, T, RWKV_WIDTH)
        out = (yn + bonus) * g
        return out.astype(rw.dtype), rw[:, -1], S_fin

    def run_group(x, c, pos, attend, shift_prev, wkv_prev, conv_prev):
        B, T, _ = x.shape
        mod = jax.nn.silu(c) @ w_ada + b_ada
        sh_m, sc_m, gt_m, sh_f, sc_f, gt_f = jnp.split(mod[:, None, :], 6, axis=-1)

        h = rms_norm(x, g_pre_mix) * (1.0 + sc_m) + sh_m
        proj = h @ w_in
        q_lat, kv_lat, k_rope, rw = jnp.split(proj, [Q_LORA, Q_LORA + KV_LORA, MLA_PROJ], axis=-1)
        cos, sin = rope_tables(pos)
        q = (rms_norm(q_lat, g_q_latent) @ w_uq).reshape(B, T, MLA_HEADS, QK_NOPE + ROPE_DIM)
        q_nope, q_pe = jnp.split(q, [QK_NOPE], axis=-1)
        q_pe = apply_rope(q_pe, cos[:, None, :], sin[:, None, :])
        q_abs = jnp.einsum('bthn,chn->bthc', q_nope, w_uk)
        kv_c = rms_norm(kv_lat, g_kv_latent)
        k_pe = apply_rope(k_rope, cos, sin)
        o_lat = attend(q_abs, q_pe, kv_c, k_pe)
        o_mla = jnp.einsum('bthc,chv->bthv', o_lat, w_uv).reshape(B, T, MLA_HEADS * V_DIM)
        o_rwkv, new_shift, new_wkv = rwkv7(rw, shift_prev, wkv_prev)
        m = jnp.concatenate([o_mla, o_rwkv], axis=-1) @ w_out
        x = x + gt_m * rms_norm(m, g_post_mix)

        h2 = rms_norm(x, g_pre_ffn) * (1.0 + sc_f) + sh_f
        gate, up = jnp.split(h2 @ w_ffn_in, 2, axis=-1)
        gate_ext = jnp.concatenate([conv_prev.astype(gate.dtype), gate], axis=1)
        gate_c = conv_b + conv_w[0] * gate_ext[:, 0:T]
        for j in range(1, CONV_W):
            gate_c = gate_c + conv_w[j] * gate_ext[:, j:j + T]
        f = (jax.nn.gelu(gate_c) * up) @ w_ffn_out
        x = x + gt_f * rms_norm(f, g_post_ffn)
        return x, kv_c, k_pe, new_shift, new_wkv, gate_ext[:, T:]

    Bp, Tp, _ = x_prompt.shape
    (y_prompt, kv_latent_prompt, k_rope_prompt, rwkv_shift_prompt,
     rwkv_wkv_prompt, ffn_conv_prompt) = run_group(
        x_prompt, c_prompt, jnp.arange(Tp), attend_prompt,
        jnp.zeros((Bp, RWKV_PROJ), x_prompt.dtype),
        jnp.zeros((Bp, RWKV_HEADS, RWKV_HEAD, RWKV_HEAD), f32),
        jnp.zeros((Bp, CONV_W - 1, D_FF), x_prompt.dtype))

    Ts = x_sample.shape[1]
    past_len = page_table.shape[1] * cache_kv_latent.shape[1]
    (y_sample, kv_latent_sample, k_rope_sample, rwkv_shift_sample,
     rwkv_wkv_sample, ffn_conv_sample) = run_group(
        x_sample, c_sample, past_len + jnp.arange(Ts), attend_sample,
        state_rwkv_shift, state_rwkv_wkv, state_ffn_conv)

    return (y_prompt, y_sample,
            kv_latent_prompt, k_rope_prompt, rwkv_shift_prompt, rwkv_wkv_prompt, ffn_conv_prompt,
            kv_latent_sample, k_rope_sample, rwkv_shift_sample, rwkv_wkv_sample, ffn_conv_sample)
```

```python
import functools

import jax
import jax.numpy as jnp
from jax import lax
from jax.experimental import pallas as pl
from jax.experimental.pallas import tpu as pltpu

F32 = jnp.float32
BF16 = jnp.bfloat16

D_MODEL = 4096
MLA_HEADS = 16
QK_NOPE = 128
ROPE_DIM = 64
V_DIM = 128
Q_LORA = 896
KV_LORA = 512
ROPE_THETA = 10000.0
ATTN_SCALE = (QK_NOPE + ROPE_DIM) ** -0.5
RWKV_HEAD = 64
RWKV_WIDTH = D_MODEL - MLA_HEADS * V_DIM
RWKV_HEADS = RWKV_WIDTH // RWKV_HEAD
DECAY_LORA = 128
ICLR_LORA = 128
GATE_LORA = 480
RWKV_PROJ = 3 * RWKV_WIDTH + DECAY_LORA + ICLR_LORA + GATE_LORA
MLA_PROJ = Q_LORA + KV_LORA + ROPE_DIM
D_FF = 11008
CONV_W = 3
NORM_EPS = 1e-6
GN_EPS = 64e-5
PAGE_SIZE = 128

LANES = 128
RW_PAD = 6912
GATE_PAD = 512
QK_CAT = KV_LORA + LANES
PAGES_PER_STEP = 8
VMEM_LIMIT = 56 * 1024 * 1024


def _params(sem):
    return pltpu.CompilerParams(dimension_semantics=sem, vmem_limit_bytes=VMEM_LIMIT)


def _rms(x, g):
    ms = jnp.mean(x * x, axis=-1, keepdims=True)
    return x * lax.rsqrt(ms + NORM_EPS) * g


def _mod_body(c_ref, w_ref, b_ref, o_ref):
    c = c_ref[...]
    s = c * jax.nn.sigmoid(c)
    o_ref[...] = jnp.dot(s.astype(BF16), w_ref[...].astype(BF16),
                         preferred_element_type=F32) + b_ref[...]


def _adaln_mod(c_all, w_ada, b_ada):
    m, d = c_all.shape
    n = w_ada.shape[1]
    tn = 512
    return pl.pallas_call(
        _mod_body,
        grid=(n // tn,),
        in_specs=[pl.BlockSpec((m, d), lambda j: (0, 0)),
                  pl.BlockSpec((d, tn), lambda j: (0, j)),
                  pl.BlockSpec((1, tn), lambda j: (0, j))],
        out_specs=pl.BlockSpec((m, tn), lambda j: (0, j)),
        out_shape=jax.ShapeDtypeStruct((m, n), F32),
        compiler_params=_params(("arbitrary",)),
        name="adaln_mod",
    )(c_all, w_ada, b_ada.reshape(1, n))


def _prenorm_body(x_ref, sc_ref, sh_ref, g_ref, o_ref):
    tb, tt, d = x_ref.shape
    y = _rms(x_ref[...], g_ref[...])
    h = y * (1.0 + sc_ref[...]) + sh_ref[...]
    o_ref[...] = h.reshape(tb * tt, d).astype(BF16)


def _prenorm(x, mod3, sc_idx, sh_idx, g, tb, tt):
    b, t, d = x.shape
    nt = t // tt
    return pl.pallas_call(
        _prenorm_body,
        grid=(b // tb, nt),
        in_specs=[pl.BlockSpec((tb, tt, d), lambda i, j: (i, j, 0)),
                  pl.BlockSpec((tb, 1, d), lambda i, j: (i, 0, sc_idx)),
                  pl.BlockSpec((tb, 1, d), lambda i, j: (i, 0, sh_idx)),
                  pl.BlockSpec((1, d), lambda i, j: (0, 0))],
        out_specs=pl.BlockSpec((tb * tt, d), lambda i, j: (i * nt + j, 0)),
        out_shape=jax.ShapeDtypeStruct((b * t, d), BF16),
        compiler_params=_params(("arbitrary", "arbitrary")),
        name="prenorm",
    )(x, mod3, mod3, g.reshape(1, d))


def _mm_body(a_ref, w_ref, o_ref):
    o_ref[...] = jnp.dot(a_ref[...], w_ref[...], preferred_element_type=F32).astype(o_ref.dtype)


def _matmul(a, w, tm, tn, out_dtype=F32):
    n, k = a.shape
    m = w.shape[1]
    return pl.pallas_call(
        _mm_body,
        grid=(n // tm, m // tn),
        in_specs=[pl.BlockSpec((tm, k), lambda i, j: (i, 0)),
                  pl.BlockSpec((k, tn), lambda i, j: (0, j))],
        out_specs=pl.BlockSpec((tm, tn), lambda i, j: (i, j)),
        out_shape=jax.ShapeDtypeStruct((n, m), out_dtype),
        compiler_params=_params(("arbitrary", "arbitrary")),
        name="matmul",
    )(a, w)


def _qprep_body(x_ref, g_ref, wq_ref, wuk_ref, cos_ref, sin_ref, o_ref):
    tb, tt, _ = x_ref.shape
    rows = tb * tt
    xn = _rms(x_ref[...], g_ref[...]).reshape(rows, Q_LORA).astype(BF16)
    q = jnp.dot(xn, wq_ref[...], preferred_element_type=F32)
    cos = cos_ref[...]
    sin = sin_ref[...]
    hw = MLA_HEADS * LANES
    for h in range(MLA_HEADS):
        qn = q[:, h * LANES:(h + 1) * LANES].astype(BF16)
        qa = jnp.dot(qn, wuk_ref[h], preferred_element_type=F32)
        o_ref[:, h, :, 0:KV_LORA] = qa.reshape(tb, tt, KV_LORA).astype(o_ref.dtype)
        pe = q[:, hw + h * LANES: hw + (h + 1) * LANES].reshape(tb, tt, LANES)
        pes = q[:, 2 * hw + h * LANES: 2 * hw + (h + 1) * LANES].reshape(tb, tt, LANES)
        o_ref[:, h, :, KV_LORA:QK_CAT] = (pe * cos + pes * sin).astype(o_ref.dtype)


def _qprep(q_lat, g_q, wq_all, wuk_t, cos, sin, tb, tt, out_dtype):
    b, t, _ = q_lat.shape
    return pl.pallas_call(
        _qprep_body,
        grid=(b // tb, t // tt),
        in_specs=[pl.BlockSpec((tb, tt, Q_LORA), lambda i, j: (i, j, 0)),
                  pl.BlockSpec((1, Q_LORA), lambda i, j: (0, 0)),
                  pl.BlockSpec(wq_all.shape, lambda i, j: (0, 0)),
                  pl.BlockSpec(wuk_t.shape, lambda i, j: (0, 0, 0)),
                  pl.BlockSpec((1, tt, LANES), lambda i, j: (0, j, 0)),
                  pl.BlockSpec((1, tt, LANES), lambda i, j: (0, j, 0))],
        out_specs=pl.BlockSpec((tb, MLA_HEADS, tt, QK_CAT), lambda i, j: (i, 0, j, 0)),
        out_shape=jax.ShapeDtypeStruct((b, MLA_HEADS, t, QK_CAT), out_dtype),
        compiler_params=_params(("arbitrary", "arbitrary")),
        name="q_prep",
    )(q_lat, g_q.reshape(1, Q_LORA), wq_all, wuk_t, cos, sin)


def _kvprep_body(x_ref, g_ref, cos_ref, sin_ref, kvc_ref, kpe_ref, kcat_ref):
    x = x_ref[...]
    kvc = _rms(x[..., 0:KV_LORA], g_ref[...])
    kpe = (x[..., KV_LORA:KV_LORA + LANES] * cos_ref[...]
           + x[..., KV_LORA + LANES:KV_LORA + 2 * LANES] * sin_ref[...])
    kvc_ref[...] = kvc
    kpe_ref[...] = kpe[..., 0:ROPE_DIM]
    kcat_ref[..., 0:KV_LORA] = kvc.astype(kcat_ref.dtype)
    kcat_ref[..., KV_LORA:QK_CAT] = kpe.astype(kcat_ref.dtype)


def _kvprep(kvr, g_kv, cos, sin, tb, tt, cat_dtype):
    b, t, w = kvr.shape
    blk = lambda n: pl.BlockSpec((tb, tt, n), lambda i, j: (i, j, 0))
    return pl.pallas_call(
        _kvprep_body,
        grid=(b // tb, t // tt),
        in_specs=[blk(w),
                  pl.BlockSpec((1, KV_LORA), lambda i, j: (0, 0)),
                  pl.BlockSpec((1, tt, LANES), lambda i, j: (0, j, 0)),
                  pl.BlockSpec((1, tt, LANES), lambda i, j: (0, j, 0))],
        out_specs=[blk(KV_LORA), blk(ROPE_DIM), blk(QK_CAT)],
        out_shape=[jax.ShapeDtypeStruct((b, t, KV_LORA), F32),
                   jax.ShapeDtypeStruct((b, t, ROPE_DIM), F32),
                   jax.ShapeDtypeStruct((b, t, QK_CAT), cat_dtype)],
        compiler_params=_params(("arbitrary", "arbitrary")),
        name="kv_prep",
    )(kvr, g_kv.reshape(1, KV_LORA), cos, sin)


ATT_TQ = 256
ATT_TK = 256
ATT_HG = 8


def _attn_prompt_body(q_ref, k_ref, o_ref, m_ref, l_ref, acc_ref):
    qi = pl.program_id(1)
    rows = ATT_HG * ATT_TQ
    q = q_ref[0].reshape(rows, QK_CAT)
    m_ref[...] = jnp.full(m_ref.shape, -jnp.inf, F32)
    l_ref[...] = jnp.zeros(l_ref.shape, F32)
    acc_ref[...] = jnp.zeros(acc_ref.shape, F32)
    qpos = qi * ATT_TQ + (lax.broadcasted_iota(jnp.int32, (rows, ATT_TK), 0) & (ATT_TQ - 1))
    col = lax.broadcasted_iota(jnp.int32, (rows, ATT_TK), 1)

    def body(kb, carry):
        kblk = k_ref[0, pl.ds(pl.multiple_of(kb * ATT_TK, ATT_TK), ATT_TK), :]
        s = lax.dot_general(q, kblk, (((1,), (1,)), ((), ())),
                            preferred_element_type=F32) * ATTN_SCALE
        s = jnp.where(kb * ATT_TK + col <= qpos, s, -jnp.inf)
        m_prev = m_ref[...]
        m_new = jnp.maximum(m_prev, jnp.max(s, axis=-1, keepdims=True))
        alpha = jnp.exp(m_prev - m_new)
        p = jnp.exp(s - m_new)
        l_ref[...] = alpha * l_ref[...] + jnp.sum(p, axis=-1, keepdims=True)
        acc_ref[...] = alpha * acc_ref[...] + jnp.dot(
            p.astype(BF16), kblk[:, 0:KV_LORA], preferred_element_type=F32)
        m_ref[...] = m_new
        return carry

    lax.fori_loop(0, qi + 1, body, 0)
    o = acc_ref[...] / l_ref[...]
    o_ref[0] = o.reshape(ATT_HG, ATT_TQ, KV_LORA).astype(o_ref.dtype)


def _attn_prompt(qcat, kcat):
    b, h, t, _ = qcat.shape
    rows = ATT_HG * ATT_TQ
    return pl.pallas_call(
        _attn_prompt_body,
        grid=(b, t // ATT_TQ, h // ATT_HG),
        in_specs=[pl.BlockSpec((1, ATT_HG, ATT_TQ, QK_CAT), lambda i, j, g: (i, g, j, 0)),
                  pl.BlockSpec((1, t, QK_CAT), lambda i, j, g: (i, 0, 0))],
        out_specs=pl.BlockSpec((1, ATT_HG, ATT_TQ, KV_LORA), lambda i, j, g: (i, g, j, 0)),
        out_shape=jax.ShapeDtypeStruct((b, h, t, KV_LORA), BF16),
        scratch_shapes=[pltpu.VMEM((rows, 1), F32), pltpu.VMEM((rows, 1), F32),
                        pltpu.VMEM((rows, KV_LORA), F32)],
        compiler_params=_params(("arbitrary", "arbitrary", "arbitrary")),
        name="attn_prompt",
    )(qcat, kcat)


def _attn_sample_body(pt_ref, q_ref, knew_ref, *refs):
    del pt_ref
    np_ = PAGES_PER_STEP
    kc_refs = refs[0:np_]
    kp_refs = refs[np_:2 * np_]
    o_ref, m_ref, l_ref, acc_ref = refs[2 * np_:2 * np_ + 4]
    g = pl.program_id(1)
    ng = pl.num_programs(1)
    ts = q_ref.shape[2]
    rows = MLA_HEADS * ts

    @pl.when(g == 0)
    def _():
        m_ref[...] = jnp.full(m_ref.shape, -jnp.inf, F32)
        l_ref[...] = jnp.zeros(l_ref.shape, F32)
        acc_ref[...] = jnp.zeros(acc_ref.shape, F32)

    q = q_ref[0].reshape(rows, QK_CAT).astype(BF16)
    qa = q[:, 0:KV_LORA]
    qp = q[:, KV_LORA:KV_LORA + ROPE_DIM]
    nt = (((1,), (1,)), ((), ()))

    def update(s, vals):
        m_prev = m_ref[...]
        m_new = jnp.maximum(m_prev, jnp.max(s, axis=-1, keepdims=True))
        alpha = jnp.exp(m_prev - m_new)
        p = jnp.exp(s - m_new)
        l_ref[...] = alpha * l_ref[...] + jnp.sum(p, axis=-1, keepdims=True)
        acc_ref[...] = alpha * acc_ref[...] + jnp.dot(p.astype(BF16), vals,
                                                      preferred_element_type=F32)
        m_ref[...] = m_new

    ss = []
    kcs = []
    for j in range(np_):
        kc = kc_refs[j][0].astype(BF16)
        kp = kp_refs[j][0].astype(BF16)
        s = (lax.dot_general(qa, kc, nt, preferred_element_type=F32)
             + lax.dot_general(qp, kp, nt, preferred_element_type=F32))
        ss.append(s * ATTN_SCALE)
        kcs.append(kc)
    update(jnp.concatenate(ss, axis=1), jnp.concatenate(kcs, axis=0))

    @pl.when(g == ng - 1)
    def _():
        knew = knew_ref[0]
        knew = jnp.concatenate([knew, jnp.zeros_like(knew)], axis=0).astype(BF16)
        s = lax.dot_general(q, knew, nt, preferred_element_type=F32) * ATTN_SCALE
        tq = lax.broadcasted_iota(jnp.int32, s.shape, 0) & (ts - 1)
        tk = lax.broadcasted_iota(jnp.int32, s.shape, 1)
        s = jnp.where(tk <= tq, s, -jnp.inf)
        update(s, knew[:, 0:KV_LORA])
        o = acc_ref[...] / l_ref[...]
        o_ref[0] = o.reshape(MLA_HEADS, ts, KV_LORA).astype(o_ref.dtype)


def _attn_sample(page_table, qcat, kcat_new, cache_kv, cache_kr):
    b, h, ts, _ = qcat.shape
    n_pages = page_table.shape[1]
    np_ = PAGES_PER_STEP
    rows = h * ts

    def page_spec(width, j):
        return pl.BlockSpec((1, PAGE_SIZE, width),
                            lambda i, g, pt: (pt[i, g * np_ + j], 0, 0))

    grid_spec = pltpu.PrefetchScalarGridSpec(
        num_scalar_prefetch=1,
        grid=(b, n_pages // np_),
        in_specs=([pl.BlockSpec((1, h, ts, QK_CAT), lambda i, g, pt: (i, 0, 0, 0)),
                   pl.BlockSpec((1, ts, QK_CAT), lambda i, g, pt: (i, 0, 0))]
                  + [page_spec(KV_LORA, j) for j in range(np_)]
                  + [page_spec(ROPE_DIM, j) for j in range(np_)]),
        out_specs=pl.BlockSpec((1, h, ts, KV_LORA), lambda i, g, pt: (i, 0, 0, 0)),
        scratch_shapes=[pltpu.VMEM((rows, 1), F32), pltpu.VMEM((rows, 1), F32),
                        pltpu.VMEM((rows, KV_LORA), F32)],
    )
    return pl.pallas_call(
        _attn_sample_body,
        grid_spec=grid_spec,
        out_shape=jax.ShapeDtypeStruct((b, h, ts, KV_LORA), F32),
        compiler_params=_params(("arbitrary", "arbitrary")),
        name="attn_sample",
    )(page_table, qcat, kcat_new, *([cache_kv] * np_), *([cache_kr] * np_))


def _mla_out_body(o_ref, w_ref, out_ref):
    tb, _, tt, c = o_ref.shape
    o = o_ref[:, 0].reshape(tb * tt, c).astype(BF16)
    out_ref[...] = jnp.dot(o, w_ref[0], preferred_element_type=F32).astype(out_ref.dtype)


def _mla_out(o_lat, wuv_t, tb, tt):
    b, h, t, c = o_lat.shape
    nt = t // tt
    return pl.pallas_call(
        _mla_out_body,
        grid=(b // tb, nt, h),
        in_specs=[pl.BlockSpec((tb, 1, tt, c), lambda i, j, k: (i, k, j, 0)),
                  pl.BlockSpec((1, c, V_DIM), lambda i, j, k: (k, 0, 0))],
        out_specs=pl.BlockSpec((tb * tt, V_DIM), lambda i, j, k: (i * nt + j, k)),
        out_shape=jax.ShapeDtypeStruct((b * t, h * V_DIM), BF16),
        compiler_params=_params(("arbitrary", "arbitrary", "arbitrary")),
        name="mla_out",
    )(o_lat, wuv_t)


def _rwprep_body(x_ref, st_ref, mu_ref, w0_ref, a0_ref, wd_ref, wa_ref, wg_ref,
                 o5_ref, g_ref, sh_ref, *carry):
    ti = pl.program_id(1)
    tb, tt, c = x_ref.shape
    rows = tb * tt
    x = x_ref[...]
    last = x[:, tt - 1:tt, :]
    if carry:
        first = jnp.where(ti == 0, st_ref[...], carry[0][...])
        carry[0][...] = last
    else:
        first = st_ref[...]
    tpos = lax.broadcasted_iota(jnp.int32, x.shape, 1)
    prev = jnp.where(tpos == 0, first, pltpu.roll(x, 1, axis=1))
    sh_ref[...] = last
    mixed = x + (prev - x) * mu_ref[...]
    w = RWKV_WIDTH
    o5_ref[0] = mixed[..., 0:w]
    o5_ref[1] = mixed[..., w:2 * w]
    o5_ref[2] = mixed[..., 2 * w:3 * w]
    wd = mixed[..., 3 * w:3 * w + DECAY_LORA].reshape(rows, DECAY_LORA)
    ad = mixed[..., 3 * w + DECAY_LORA:3 * w + DECAY_LORA + ICLR_LORA].reshape(rows, ICLR_LORA)
    gd = mixed[..., 3 * w + DECAY_LORA + ICLR_LORA:c].reshape(rows, GATE_PAD)
    z = w0_ref[...] + jnp.dot(jnp.tanh(wd).astype(BF16), wd_ref[...], preferred_element_type=F32)
    w_log = -jax.nn.softplus(-z) - 0.5
    o5_ref[3] = jnp.exp(-jnp.exp(w_log)).reshape(tb, tt, w)
    a = jax.nn.sigmoid(a0_ref[...] + jnp.dot(ad.astype(BF16), wa_ref[...],
                                             preferred_element_type=F32))
    o5_ref[4] = a.reshape(tb, tt, w)
    g = jnp.dot(jax.nn.sigmoid(gd).astype(BF16), wg_ref[...], preferred_element_type=F32)
    g_ref[...] = g.reshape(tb, tt, w)


def _rwprep(rw, shift_prev, mu_p, w0, a0, wd, wa, wg, tb, tt):
    b, t, c = rw.shape
    w = RWKV_WIDTH
    full = lambda a: pl.BlockSpec(a.shape, lambda i, j: (0,) * a.ndim)
    mu_p = mu_p.reshape(1, c)
    w0 = w0.reshape(1, w)
    a0 = a0.reshape(1, w)
    return pl.pallas_call(
        _rwprep_body,
        grid=(b // tb, t // tt),
        in_specs=[pl.BlockSpec((tb, tt, c), lambda i, j: (i, j, 0)),
                  pl.BlockSpec((tb, 1, c), lambda i, j: (i, 0, 0)),
                  full(mu_p), full(w0), full(a0), full(wd), full(wa), full(wg)],
        out_specs=[pl.BlockSpec((5, tb, tt, w), lambda i, j: (0, i, j, 0)),
                   pl.BlockSpec((tb, tt, w), lambda i, j: (i, j, 0)),
                   pl.BlockSpec((tb, 1, c), lambda i, j: (i, 0, 0))],
        out_shape=[jax.ShapeDtypeStruct((5, b, t, w), F32),
                   jax.ShapeDtypeStruct((b, t, w), F32),
                   jax.ShapeDtypeStruct((b, 1, c), F32)],
        scratch_shapes=[pltpu.VMEM((tb, 1, c), F32)] if t > tt else [],
        compiler_params=_params(("arbitrary", "arbitrary")),
        name="rwkv_prep",
    )(rw, shift_prev, mu_p, w0, a0, wd, wa, wg)


def _scan_body(x_ref, s0_ref, kkp_ref, kap_ref, rkp_ref, lnw_ref, lnb_ref,
               o_ref, sfin_ref, s_ref, kk_ref, b_ref, kh_ref):
    ti = pl.program_id(1)
    tc = x_ref.shape[2]
    n = RWKV_HEAD

    @pl.when(ti == 0)
    def _():
        s_ref[...] = s0_ref[0]

    r = x_ref[0, 0]
    k = x_ref[1, 0]
    v = x_ref[2, 0]
    a = x_ref[4, 0]
    kk = k * kkp_ref[...]
    kk = kk / jnp.maximum(jnp.sqrt(jnp.sum(kk * kk, axis=1, keepdims=True)), 1e-12)
    kh = k * (1.0 + (a - 1.0) * kap_ref[...])
    kk_ref[...] = kk
    b_ref[...] = kk * a
    kh_ref[...] = kh

    def step(t, carry):
        sa = jnp.zeros((n, LANES), F32)
        for i in range(n):
            sa = sa + s_ref[i] * kk_ref[t, pl.ds(i, 1), :]
        sa = -sa
        vt = x_ref[2, 0, t]
        y = jnp.zeros((n, LANES), F32)
        for i in range(n):
            s_new = (s_ref[i] * x_ref[3, 0, t, pl.ds(i, 1), :]
                     + sa * b_ref[t, pl.ds(i, 1), :]
                     + vt * kh_ref[t, pl.ds(i, 1), :])
            s_ref[i] = s_new
            y = y + s_new * x_ref[0, 0, t, pl.ds(i, 1), :]
        o_ref[0, t] = y
        return carry

    lax.fori_loop(0, tc, step, 0)

    y = o_ref[0]
    mu = jnp.mean(y, axis=1, keepdims=True)
    var = jnp.mean(jnp.square(y - mu), axis=1, keepdims=True)
    yn = (y - mu) * lax.rsqrt(var + GN_EPS) * lnw_ref[...] + lnb_ref[...]
    bonus = jnp.sum(r * kh * rkp_ref[...], axis=1, keepdims=True) * v
    o_ref[0] = yn + bonus

    @pl.when(ti == pl.num_programs(1) - 1)
    def _():
        sfin_ref[0] = s_ref[...]


def _scan(x5, s0, kkp, kap, rkp, lnw, lnb, tc):
    _, g, t, n, l = x5.shape
    par = pl.BlockSpec((n, l), lambda i, j: (0, 0))
    return pl.pallas_call(
        _scan_body,
        grid=(g, t // tc),
        in_specs=[pl.BlockSpec((5, 1, tc, n, l), lambda i, j: (0, i, j, 0, 0)),
                  pl.BlockSpec((1, n, n, l), lambda i, j: (i, 0, 0, 0)),
                  par, par, par, par, par],
        out_specs=[pl.BlockSpec((1, tc, n, l), lambda i, j: (i, j, 0, 0)),
                   pl.BlockSpec((1, n, n, l), lambda i, j: (i, 0, 0, 0))],
        out_shape=[jax.ShapeDtypeStruct((g, t, n, l), F32),
                   jax.ShapeDtypeStruct((g, n, n, l), F32)],
        scratch_shapes=[pltpu.VMEM((n, n, l), F32), pltpu.VMEM((tc, n, l), F32),
                        pltpu.VMEM((tc, n, l), F32), pltpu.VMEM((tc, n, l), F32)],
        compiler_params=_params(("arbitrary", "arbitrary")),
        name="rwkv_scan",
    )(x5, s0, kkp, kap, rkp, lnw, lnb)


OUT_TK = 512


def _outproj_body(am_ref, op_ref, g_ref, w_ref, x_ref, gt_ref, gp_ref, o_ref):
    k = pl.program_id(2)
    nk = pl.num_programs(2)
    tb, tt, d = x_ref.shape

    @pl.when(k == 0)
    def _():
        o_ref[...] = jnp.zeros(o_ref.shape, F32)

    @pl.when(k < nk // 2)
    def _():
        o_ref[...] += jnp.dot(am_ref[...], w_ref[...],
                              preferred_element_type=F32).reshape(tb, tt, d)

    @pl.when(k >= nk // 2)
    def _():
        a = (op_ref[...] * g_ref[...]).astype(BF16)
        o_ref[...] += jnp.dot(a, w_ref[...], preferred_element_type=F32).reshape(tb, tt, d)

    @pl.when(k == nk - 1)
    def _():
        o_ref[...] = x_ref[...] + gt_ref[...] * _rms(o_ref[...], gp_ref[...])


def _outproj(o_mla, o_pre, g, w_out, x, mod3, gt_idx, g_post, tb, tt):
    b, t, d = x.shape
    nt = t // tt
    rows = tb * tt
    nk = d // OUT_TK
    half = nk // 2
    return pl.pallas_call(
        _outproj_body,
        grid=(b // tb, nt, nk),
        in_specs=[pl.BlockSpec((rows, OUT_TK), lambda i, j, k: (i * nt + j, jnp.minimum(k, half - 1))),
                  pl.BlockSpec((rows, OUT_TK), lambda i, j, k: (i * nt + j, jnp.maximum(k - half, 0))),
                  pl.BlockSpec((rows, OUT_TK), lambda i, j, k: (i * nt + j, jnp.maximum(k - half, 0))),
                  pl.BlockSpec((OUT_TK, d), lambda i, j, k: (k, 0)),
                  pl.BlockSpec((tb, tt, d), lambda i, j, k: (i, j, 0)),
                  pl.BlockSpec((tb, 1, d), lambda i, j, k: (i, 0, gt_idx)),
                  pl.BlockSpec((1, d), lambda i, j, k: (0, 0))],
        out_specs=pl.BlockSpec((tb, tt, d), lambda i, j, k: (i, j, 0)),
        out_shape=jax.ShapeDtypeStruct((b, t, d), F32),
        compiler_params=_params(("arbitrary", "arbitrary", "arbitrary")),
        name="out_proj",
    )(o_mla, o_pre, g, w_out, x, mod3, g_post.reshape(1, d))


FFN_TF = 256


def _ffn_body(h_ref, wg_ref, wu_ref, wo_ref, cw_ref, cb_ref, st_ref, f_ref, cs_ref, *carry,
              tb, tt):
    ti = pl.program_id(1)
    j = pl.program_id(2)
    tf = wg_ref.shape[1]
    h = h_ref[...]
    gate = jnp.dot(h, wg_ref[...], preferred_element_type=F32).reshape(tb, tt, tf)
    up = jnp.dot(h, wu_ref[...], preferred_element_type=F32).reshape(tb, tt, tf)
    last2 = gate[:, tt - 2:tt, :]
    if carry:
        prev2 = jnp.where(ti == 0, st_ref[...], carry[0][j])
        carry[0][j] = last2
    else:
        prev2 = st_ref[...]
    p0 = prev2[:, 0:1, :]
    p1 = prev2[:, 1:2, :]
    tpos = lax.broadcasted_iota(jnp.int32, gate.shape, 1)
    g1 = jnp.where(tpos == 0, p1, pltpu.roll(gate, 1, axis=1))
    g2 = jnp.where(tpos == 0, p0, jnp.where(tpos == 1, p1, pltpu.roll(gate, 2, axis=1)))
    cs_ref[...] = last2
    cw = cw_ref[...]
    gate_c = cb_ref[...] + cw[0:1, :] * g2
    gate_c = gate_c + cw[1:2, :] * g1
    gate_c = gate_c + cw[2:3, :] * gate
    act = (jax.nn.gelu(gate_c) * up).reshape(tb * tt, tf).astype(BF16)
    contrib = jnp.dot(act, wo_ref[...], preferred_element_type=F32)

    @pl.when(j == 0)
    def _():
        f_ref[...] = contrib

    @pl.when(j > 0)
    def _():
        f_ref[...] += contrib


def _ffn(h2, wg, wu, wo, conv_w, conv_b, conv_prev, b, t, tb, tt):
    d = h2.shape[1]
    nt = t // tt
    rows = tb * tt
    tf = FFN_TF
    nj = D_FF // tf
    return pl.pallas_call(
        functools.partial(_ffn_body, tb=tb, tt=tt),
        grid=(b // tb, nt, nj),
        in_specs=[pl.BlockSpec((rows, d), lambda i, k, j: (i * nt + k, 0)),
                  pl.BlockSpec((d, tf), lambda i, k, j: (0, j)),
                  pl.BlockSpec((d, tf), lambda i, k, j: (0, j)),
                  pl.BlockSpec((tf, d), lambda i, k, j: (j, 0)),
                  pl.BlockSpec((CONV_W, tf), lambda i, k, j: (0, j)),
                  pl.BlockSpec((1, tf), lambda i, k, j: (0, j)),
                  pl.BlockSpec((tb, CONV_W - 1, tf), lambda i, k, j: (i, 0, j))],
        out_specs=[pl.BlockSpec((rows, d), lambda i, k, j: (i * nt + k, 0)),
                   pl.BlockSpec((tb, CONV_W - 1, tf), lambda i, k, j: (i, 0, j))],
        out_shape=[jax.ShapeDtypeStruct((b * t, d), F32),
                   jax.ShapeDtypeStruct((b, CONV_W - 1, D_FF), F32)],
        scratch_shapes=[pltpu.VMEM((nj, tb, CONV_W - 1, tf), F32)] if nt > 1 else [],
        compiler_params=_params(("arbitrary", "arbitrary", "arbitrary")),
        name="conv_ffn",
    )(h2, wg, wu, wo, conv_w, conv_b.reshape(1, D_FF), conv_prev)


def _final_body(x_ref, f_ref, gt_ref, gp_ref, o_ref):
    tb, tt, d = x_ref.shape
    f = f_ref[...].reshape(tb, tt, d)
    o_ref[...] = x_ref[...] + gt_ref[...] * _rms(f, gp_ref[...])


def _final(x, f, mod3, gt_idx, g_post, tb, tt):
    b, t, d = x.shape
    nt = t // tt
    return pl.pallas_call(
        _final_body,
        grid=(b // tb, nt),
        in_specs=[pl.BlockSpec((tb, tt, d), lambda i, j: (i, j, 0)),
                  pl.BlockSpec((tb * tt, d), lambda i, j: (i * nt + j, 0)),
                  pl.BlockSpec((tb, 1, d), lambda i, j: (i, 0, gt_idx)),
                  pl.BlockSpec((1, d), lambda i, j: (0, 0))],
        out_specs=pl.BlockSpec((tb, tt, d), lambda i, j: (i, j, 0)),
        out_shape=jax.ShapeDtypeStruct((b, t, d), F32),
        compiler_params=_params(("arbitrary", "arbitrary")),
        name="residual_norm",
    )(x, f, mod3, g_post.reshape(1, d))


def _rope_tables(pos):
    half = ROPE_DIM // 2
    inv = ROPE_THETA ** (-jnp.arange(half, dtype=F32) * 2.0 / ROPE_DIM)
    ang = pos.astype(F32)[:, None] * inv[None, :]
    cos, sin = jnp.cos(ang), jnp.sin(ang)
    z = jnp.zeros((pos.shape[0], LANES - ROPE_DIM), F32)
    cos_t = jnp.concatenate([cos, cos, z], axis=-1)[None]
    sin_t = jnp.concatenate([-sin, sin, z], axis=-1)[None]
    return cos_t, sin_t


def _swap_halves(w):
    half = ROPE_DIM // 2
    return jnp.concatenate([w[..., half:], w[..., :half]], axis=-1)


def _head_lane_param(p):
    ph = p.reshape(RWKV_HEADS, RWKV_HEAD).T
    return jnp.tile(ph, (1, LANES // RWKV_HEADS))


def _to_lanes(x5, b, t):
    per = LANES // RWKV_HEADS
    g = b // per
    x = x5.reshape(5, g, per, t, RWKV_HEADS, RWKV_HEAD)
    return x.transpose(0, 1, 3, 5, 2, 4).reshape(5, g, t, RWKV_HEAD, LANES)


def _from_lanes(o, b, t):
    per = LANES // RWKV_HEADS
    g = b // per
    x = o.reshape(g, t, RWKV_HEAD, per, RWKV_HEADS)
    return x.transpose(0, 3, 1, 4, 2).reshape(b * t, RWKV_WIDTH)


def _state_to_lanes(s, b):
    per = LANES // RWKV_HEADS
    g = b // per
    x = s.reshape(g, per, RWKV_HEADS, RWKV_HEAD, RWKV_HEAD)
    return x.transpose(0, 4, 3, 1, 2).reshape(g, RWKV_HEAD, RWKV_HEAD, LANES)


def _state_from_lanes(s, b):
    per = LANES // RWKV_HEADS
    g = b // per
    x = s.reshape(g, RWKV_HEAD, RWKV_HEAD, per, RWKV_HEADS)
    return x.transpose(0, 3, 4, 2, 1).reshape(b, RWKV_HEADS, RWKV_HEAD, RWKV_HEAD)


def _prep_weights(w_in, w_uq, w_uk, w_uv, mu_shift, w_decay_up, w_iclr_up, w_gate_up,
                  w_out, w_ffn_in, w_ffn_out, k_k, k_a, r_k, ln_x_w, ln_x_b):
    d = D_MODEL
    z64 = jnp.zeros((d, LANES - ROPE_DIM), F32)
    w_kr = w_in[:, Q_LORA + KV_LORA:MLA_PROJ]
    wp = {}
    wp["w_q"] = w_in[:, :Q_LORA].astype(BF16)
    wp["w_kvr"] = jnp.concatenate(
        [w_in[:, Q_LORA:Q_LORA + KV_LORA], w_kr, z64, _swap_halves(w_kr), z64], axis=1).astype(BF16)
    wp["w_rw"] = jnp.pad(w_in[:, MLA_PROJ:], ((0, 0), (0, RW_PAD - RWKV_PROJ))).astype(BF16)
    wq3 = w_uq.reshape(Q_LORA, MLA_HEADS, QK_NOPE + ROPE_DIM)
    pe = wq3[:, :, QK_NOPE:]
    padl = ((0, 0), (0, 0), (0, LANES - ROPE_DIM))
    wp["wq_all"] = jnp.concatenate(
        [wq3[:, :, :QK_NOPE].reshape(Q_LORA, -1),
         jnp.pad(pe, padl).reshape(Q_LORA, -1),
         jnp.pad(_swap_halves(pe), padl).reshape(Q_LORA, -1)], axis=1).astype(BF16)
    wp["wuk_t"] = w_uk.transpose(1, 2, 0).astype(BF16)
    wp["wuv_t"] = w_uv.transpose(1, 0, 2).astype(BF16)
    wp["mu"] = jnp.pad(mu_shift, (0, RW_PAD - RWKV_PROJ))
    wp["wd"] = w_decay_up.astype(BF16)
    wp["wa"] = w_iclr_up.astype(BF16)
    wp["wg"] = jnp.pad(w_gate_up, ((0, GATE_PAD - GATE_LORA), (0, 0))).astype(BF16)
    wp["w_out"] = w_out.astype(BF16)
    wp["w_gate"] = w_ffn_in[:, :D_FF].astype(BF16)
    wp["w_up"] = w_ffn_in[:, D_FF:].astype(BF16)
    wp["w_ffn_out"] = w_ffn_out.astype(BF16)
    wp["kkp"] = _head_lane_param(k_k)
    wp["kap"] = _head_lane_param(k_a)
    wp["rkp"] = _head_lane_param(r_k.reshape(-1))
    wp["lnw"] = _head_lane_param(ln_x_w)
    wp["lnb"] = _head_lane_param(ln_x_b)
    return wp


def _run_group(x, mod3, pos, wp, prm, shift_prev, wkv_prev, conv_prev, tiles, attend):
    b, t, d = x.shape
    tb, tt = tiles["row"]
    cos, sin = _rope_tables(pos)

    h = _prenorm(x, mod3, 1, 0, prm["g_pre_mix"], tb, tt)
    tm = tiles["mm_rows"]
    q_lat = _matmul(h, wp["w_q"], tm, Q_LORA).reshape(b, t, Q_LORA)
    kvr = _matmul(h, wp["w_kvr"], tm, wp["w_kvr"].shape[1]).reshape(b, t, -1)
    rw = _matmul(h, wp["w_rw"], tm, 768).reshape(b, t, RW_PAD)

    qtb, qtt = tiles["q"]
    qcat = _qprep(q_lat, prm["g_q_latent"], wp["wq_all"], wp["wuk_t"], cos, sin, qtb, qtt,
                  tiles["cat_dtype"])
    kv_c, k_pe, kcat = _kvprep(kvr, prm["g_kv_latent"], cos, sin, tb, tt, tiles["cat_dtype"])
    o_lat = attend(qcat, kcat)
    otb, ott = tiles["o"]
    o_mla = _mla_out(o_lat, wp["wuv_t"], otb, ott)

    rtb, rtt = tiles["rw"]
    x5, g, new_shift = _rwprep(rw, shift_prev, wp["mu"], prm["w0"], prm["a0"],
                               wp["wd"], wp["wa"], wp["wg"], rtb, rtt)
    o_scan, s_fin = _scan(_to_lanes(x5, b, t), _state_to_lanes(wkv_prev, b),
                          wp["kkp"], wp["kap"], wp["rkp"], wp["lnw"], wp["lnb"], tiles["scan_tc"])
    o_pre = _from_lanes(o_scan, b, t)
    new_wkv = _state_from_lanes(s_fin, b)

    x1 = _outproj(o_mla, o_pre, g.reshape(b * t, RWKV_WIDTH), wp["w_out"], x, mod3, 2,
                  prm["g_post_mix"], tb, tt)

    h2 = _prenorm(x1, mod3, 4, 3, prm["g_pre_ffn"], tb, tt)
    ftb, ftt = tiles["ffn"]
    f, conv_state = _ffn(h2, wp["w_gate"], wp["w_up"], wp["w_ffn_out"], prm["conv_w"],
                         prm["conv_b"], conv_prev, b, t, ftb, ftt)
    y = _final(x1, f, mod3, 5, prm["g_post_ffn"], tb, tt)
    return y, kv_c, k_pe, new_shift[:, 0, :RWKV_PROJ], new_wkv, conv_state


def kernel(x_prompt, x_sample, c_prompt, c_sample, cache_kv_latent, cache_k_rope, page_table,
           state_rwkv_shift, state_rwkv_wkv, state_ffn_conv, w_ada, b_ada, g_pre_mix, g_post_mix,
           g_pre_ffn, g_post_ffn, w_in, g_q_latent, w_uq, g_kv_latent, w_uk, w_uv, mu_shift, w0,
           w_decay_up, a0, w_iclr_up, w_gate_up, k_k, k_a, r_k, ln_x_w, ln_x_b, w_out, w_ffn_in,
           conv_w, conv_b, w_ffn_out):
    bp, tp, d = x_prompt.shape
    bs, ts, _ = x_sample.shape
    wp = _prep_weights(w_in, w_uq, w_uk, w_uv, mu_shift, w_decay_up, w_iclr_up, w_gate_up,
                       w_out, w_ffn_in, w_ffn_out, k_k, k_a, r_k, ln_x_w, ln_x_b)
    prm = dict(g_pre_mix=g_pre_mix, g_post_mix=g_post_mix, g_pre_ffn=g_pre_ffn,
               g_post_ffn=g_post_ffn, g_q_latent=g_q_latent, g_kv_latent=g_kv_latent,
               w0=w0, a0=a0, conv_w=conv_w, conv_b=conv_b)

    n_c = bp + bs
    n_c_pad = -(-n_c // 8) * 8
    c_all = jnp.concatenate([c_prompt, c_sample, jnp.zeros((n_c_pad - n_c, d), F32)], axis=0)
    mod = _adaln_mod(c_all, w_ada, b_ada)
    mod_p = mod[:bp].reshape(bp, 1, 6 * d)
    mod_s = mod[bp:n_c].reshape(bs, 1, 6 * d)

    tiles_p = dict(row=(1, 512), ffn=(1, 512), mm_rows=1024, q=(1, 256), o=(1, 1024),
                   rw=(1, 128), scan_tc=16, cat_dtype=BF16)
    outs_p = _run_group(
        x_prompt, mod_p, jnp.arange(tp), wp, prm,
        jnp.zeros((bp, 1, RW_PAD), F32),
        jnp.zeros((bp, RWKV_HEADS, RWKV_HEAD, RWKV_HEAD), F32),
        jnp.zeros((bp, CONV_W - 1, D_FF), F32),
        tiles_p, _attn_prompt)

    past_len = page_table.shape[1] * cache_kv_latent.shape[1]
    tiles_s = dict(row=(32, ts), ffn=(64, ts), mm_rows=1024, q=(16, ts), o=(bs, ts),
                   rw=(16, ts), scan_tc=ts, cat_dtype=F32)
    shift_s = jnp.pad(state_rwkv_shift, ((0, 0), (0, RW_PAD - RWKV_PROJ)))[:, None, :]
    attend_s = lambda qcat, kcat: _attn_sample(page_table, qcat, kcat, cache_kv_latent, cache_k_rope)
    outs_s = _run_group(
        x_sample, mod_s, past_len + jnp.arange(ts), wp, prm,
        shift_s, state_rwkv_wkv, state_ffn_conv, tiles_s, attend_s)

    y_p, kv_p, kr_p, sh_p, wkv_p, cs_p = outs_p
    y_s, kv_s, kr_s, sh_s, wkv_s, cs_s = outs_s
    return (y_p, y_s, kv_p, kr_p, sh_p, wkv_p, cs_p, kv_s, kr_s, sh_s, wkv_s, cs_s)
```

```python
import functools

import jax
import jax.numpy as jnp
from jax import lax
from jax.experimental import pallas as pl
from jax.experimental.pallas import tpu as pltpu

F32 = jnp.float32
BF16 = jnp.bfloat16

D_MODEL = 4096
MLA_HEADS = 16
QK_NOPE = 128
ROPE_DIM = 64
V_DIM = 128
Q_LORA = 896
KV_LORA = 512
ROPE_THETA = 10000.0
ATTN_SCALE = (QK_NOPE + ROPE_DIM) ** -0.5
RWKV_HEAD = 64
RWKV_WIDTH = D_MODEL - MLA_HEADS * V_DIM
RWKV_HEADS = RWKV_WIDTH // RWKV_HEAD
DECAY_LORA = 128
ICLR_LORA = 128
GATE_LORA = 480
RWKV_PROJ = 3 * RWKV_WIDTH + DECAY_LORA + ICLR_LORA + GATE_LORA
MLA_PROJ = Q_LORA + KV_LORA + ROPE_DIM
D_FF = 11008
CONV_W = 3
NORM_EPS = 1e-6
GN_EPS = 64e-5
PAGE_SIZE = 128

LANES = 128
RW_PAD = 6912
GATE_PAD = 512
QK_CAT = KV_LORA + LANES
VMEM_LIMIT = 56 * 1024 * 1024


def _params(sem):
    return pltpu.CompilerParams(dimension_semantics=sem, vmem_limit_bytes=VMEM_LIMIT)


def _rms(x, g):
    ms = jnp.mean(x * x, axis=-1, keepdims=True)
    return x * lax.rsqrt(ms + NORM_EPS) * g


def _mod_body(c_ref, w_ref, b_ref, o_ref):
    c = c_ref[...]
    s = c * jax.nn.sigmoid(c)
    o_ref[...] = jnp.dot(s.astype(BF16), w_ref[...].astype(BF16),
                         preferred_element_type=F32) + b_ref[...]


def _adaln_mod(c_all, w_ada, b_ada):
    m, d = c_all.shape
    n = w_ada.shape[1]
    tn = 512
    return pl.pallas_call(
        _mod_body,
        grid=(n // tn,),
        in_specs=[pl.BlockSpec((m, d), lambda j: (0, 0)),
                  pl.BlockSpec((d, tn), lambda j: (0, j)),
                  pl.BlockSpec((1, tn), lambda j: (0, j))],
        out_specs=pl.BlockSpec((m, tn), lambda j: (0, j)),
        out_shape=jax.ShapeDtypeStruct((m, n), F32),
        compiler_params=_params(("arbitrary",)),
        name="adaln_mod",
    )(c_all, w_ada, b_ada.reshape(1, n))


def _prenorm_body(x_ref, sc_ref, sh_ref, g_ref, o_ref):
    tb, tt, d = x_ref.shape
    y = _rms(x_ref[...], g_ref[...])
    h = y * (1.0 + sc_ref[...]) + sh_ref[...]
    o_ref[...] = h.reshape(tb * tt, d).astype(BF16)


def _prenorm(x, mod3, sc_idx, sh_idx, g, tb, tt):
    b, t, d = x.shape
    nt = t // tt
    return pl.pallas_call(
        _prenorm_body,
        grid=(b // tb, nt),
        in_specs=[pl.BlockSpec((tb, tt, d), lambda i, j: (i, j, 0)),
                  pl.BlockSpec((tb, 1, d), lambda i, j: (i, 0, sc_idx)),
                  pl.BlockSpec((tb, 1, d), lambda i, j: (i, 0, sh_idx)),
                  pl.BlockSpec((1, d), lambda i, j: (0, 0))],
        out_specs=pl.BlockSpec((tb * tt, d), lambda i, j: (i * nt + j, 0)),
        out_shape=jax.ShapeDtypeStruct((b * t, d), BF16),
        compiler_params=_params(("arbitrary", "arbitrary")),
        name="prenorm",
    )(x, mod3, mod3, g.reshape(1, d))


def _mm_body(a_ref, w_ref, o_ref):
    o_ref[...] = jnp.dot(a_ref[...], w_ref[...], preferred_element_type=F32).astype(o_ref.dtype)


def _matmul(a, w, tm, tn, out_dtype=F32):
    n, k = a.shape
    m = w.shape[1]
    return pl.pallas_call(
        _mm_body,
        grid=(n // tm, m // tn),
        in_specs=[pl.BlockSpec((tm, k), lambda i, j: (i, 0)),
                  pl.BlockSpec((k, tn), lambda i, j: (0, j))],
        out_specs=pl.BlockSpec((tm, tn), lambda i, j: (i, j)),
        out_shape=jax.ShapeDtypeStruct((n, m), out_dtype),
        compiler_params=_params(("arbitrary", "arbitrary")),
        name="matmul",
    )(a, w)


def _qprep_body(x_ref, g_ref, wq_ref, wuk_ref, cos_ref, sin_ref, o_ref):
    tb, tt, _ = x_ref.shape
    rows = tb * tt
    xn = _rms(x_ref[...], g_ref[...]).reshape(rows, Q_LORA).astype(BF16)
    q = jnp.dot(xn, wq_ref[...], preferred_element_type=F32)
    cos = cos_ref[...]
    sin = sin_ref[...]
    hw = MLA_HEADS * LANES
    for h in range(MLA_HEADS):
        qn = q[:, h * LANES:(h + 1) * LANES].astype(BF16)
        qa = jnp.dot(qn, wuk_ref[h], preferred_element_type=F32)
        o_ref[:, h, :, 0:KV_LORA] = qa.reshape(tb, tt, KV_LORA).astype(o_ref.dtype)
        pe = q[:, hw + h * LANES: hw + (h + 1) * LANES].reshape(tb, tt, LANES)
        pes = q[:, 2 * hw + h * LANES: 2 * hw + (h + 1) * LANES].reshape(tb, tt, LANES)
        o_ref[:, h, :, KV_LORA:QK_CAT] = (pe * cos + pes * sin).astype(o_ref.dtype)


def _qprep(q_lat, g_q, wq_all, wuk_t, cos, sin, tb, tt, out_dtype):
    b, t, _ = q_lat.shape
    return pl.pallas_call(
        _qprep_body,
        grid=(b // tb, t // tt),
        in_specs=[pl.BlockSpec((tb, tt, Q_LORA), lambda i, j: (i, j, 0)),
                  pl.BlockSpec((1, Q_LORA), lambda i, j: (0, 0)),
                  pl.BlockSpec(wq_all.shape, lambda i, j: (0, 0)),
                  pl.BlockSpec(wuk_t.shape, lambda i, j: (0, 0, 0)),
                  pl.BlockSpec((1, tt, LANES), lambda i, j: (0, j, 0)),
                  pl.BlockSpec((1, tt, LANES), lambda i, j: (0, j, 0))],
        out_specs=pl.BlockSpec((tb, MLA_HEADS, tt, QK_CAT), lambda i, j: (i, 0, j, 0)),
        out_shape=jax.ShapeDtypeStruct((b, MLA_HEADS, t, QK_CAT), out_dtype),
        compiler_params=_params(("arbitrary", "arbitrary")),
        name="q_prep",
    )(q_lat, g_q.reshape(1, Q_LORA), wq_all, wuk_t, cos, sin)


def _kvprep_body(x_ref, g_ref, cos_ref, sin_ref, kvc_ref, kpe_ref, kcat_ref):
    x = x_ref[...]
    kvc = _rms(x[..., 0:KV_LORA], g_ref[...])
    kpe = (x[..., KV_LORA:KV_LORA + LANES] * cos_ref[...]
           + x[..., KV_LORA + LANES:KV_LORA + 2 * LANES] * sin_ref[...])
    kvc_ref[...] = kvc
    kpe_ref[...] = kpe[..., 0:ROPE_DIM]
    kcat_ref[..., 0:KV_LORA] = kvc.astype(kcat_ref.dtype)
    kcat_ref[..., KV_LORA:QK_CAT] = kpe.astype(kcat_ref.dtype)


def _kvprep(kvr, g_kv, cos, sin, tb, tt, cat_dtype):
    b, t, w = kvr.shape
    blk = lambda n: pl.BlockSpec((tb, tt, n), lambda i, j: (i, j, 0))
    return pl.pallas_call(
        _kvprep_body,
        grid=(b // tb, t // tt),
        in_specs=[blk(w),
                  pl.BlockSpec((1, KV_LORA), lambda i, j: (0, 0)),
                  pl.BlockSpec((1, tt, LANES), lambda i, j: (0, j, 0)),
                  pl.BlockSpec((1, tt, LANES), lambda i, j: (0, j, 0))],
        out_specs=[blk(KV_LORA), blk(ROPE_DIM), blk(QK_CAT)],
        out_shape=[jax.ShapeDtypeStruct((b, t, KV_LORA), F32),
                   jax.ShapeDtypeStruct((b, t, ROPE_DIM), F32),
                   jax.ShapeDtypeStruct((b, t, QK_CAT), cat_dtype)],
        compiler_params=_params(("arbitrary", "arbitrary")),
        name="kv_prep",
    )(kvr, g_kv.reshape(1, KV_LORA), cos, sin)


ATT_TQ = 256
ATT_TK = 512
ATT_HG = 4


def _attn_prompt_body(q_ref, k_ref, o_ref, m_ref, l_ref, acc_ref):
    qi = pl.program_id(1)
    rows = ATT_HG * ATT_TQ
    q = q_ref[0].reshape(rows, QK_CAT)
    m_ref[...] = jnp.full(m_ref.shape, -jnp.inf, F32)
    l_ref[...] = jnp.zeros(l_ref.shape, F32)
    acc_ref[...] = jnp.zeros(acc_ref.shape, F32)
    qpos = qi * ATT_TQ + (lax.broadcasted_iota(jnp.int32, (rows, ATT_TK), 0) & (ATT_TQ - 1))
    col = lax.broadcasted_iota(jnp.int32, (rows, ATT_TK), 1)

    def body(kb, carry):
        kblk = k_ref[0, pl.ds(pl.multiple_of(kb * ATT_TK, ATT_TK), ATT_TK), :]
        s = lax.dot_general(q, kblk, (((1,), (1,)), ((), ())),
                            preferred_element_type=F32) * ATTN_SCALE
        s = jnp.where(kb * ATT_TK + col <= qpos, s, -jnp.inf)
        m_prev = m_ref[...]
        m_new = jnp.maximum(m_prev, jnp.max(s, axis=-1, keepdims=True))
        alpha = jnp.exp(m_prev - m_new)
        p = jnp.exp(s - m_new)
        l_ref[...] = alpha * l_ref[...] + jnp.sum(p, axis=-1, keepdims=True)
        acc_ref[...] = alpha * acc_ref[...] + jnp.dot(
            p.astype(BF16), kblk[:, 0:KV_LORA], preferred_element_type=F32)
        m_ref[...] = m_new
        return carry

    n_kb = ((qi + 1) * ATT_TQ + ATT_TK - 1) // ATT_TK
    lax.fori_loop(0, n_kb, body, 0)
    o = acc_ref[...] / l_ref[...]
    o_ref[0] = o.reshape(ATT_HG, ATT_TQ, KV_LORA).astype(o_ref.dtype)


def _attn_prompt(qcat, kcat):
    b, h, t, _ = qcat.shape
    rows = ATT_HG * ATT_TQ
    return pl.pallas_call(
        _attn_prompt_body,
        grid=(b, t // ATT_TQ, h // ATT_HG),
        in_specs=[pl.BlockSpec((1, ATT_HG, ATT_TQ, QK_CAT), lambda i, j, g: (i, g, j, 0)),
                  pl.BlockSpec((1, t, QK_CAT), lambda i, j, g: (i, 0, 0))],
        out_specs=pl.BlockSpec((1, ATT_HG, ATT_TQ, KV_LORA), lambda i, j, g: (i, g, j, 0)),
        out_shape=jax.ShapeDtypeStruct((b, h, t, KV_LORA), BF16),
        scratch_shapes=[pltpu.VMEM((rows, 1), F32), pltpu.VMEM((rows, 1), F32),
                        pltpu.VMEM((rows, KV_LORA), F32)],
        compiler_params=_params(("arbitrary", "arbitrary", "arbitrary")),
        name="attn_prompt",
    )(qcat, kcat)


CHUNK_PAGES = 32
SUB_KEYS = 1024


def _attn_sample_body(pt_ref, q_ref, knew_ref, ckv_hbm, ckr_hbm, o_ref,
                      kc_buf, kp_buf, sem, m_ref, l_ref, acc_ref):
    b = pl.program_id(0)
    k = pl.program_id(1)
    nb = pl.num_programs(0)
    nk = pl.num_programs(1)
    ts = q_ref.shape[2]
    rows = MLA_HEADS * ts
    step = b * nk + k
    slot = step & 1

    def chunk_copies(seq, chunk, sl):
        cps = []
        for j in range(CHUNK_PAGES):
            page = pt_ref[seq, chunk * CHUNK_PAGES + j]
            dst = pl.ds(j * PAGE_SIZE, PAGE_SIZE)
            cps.append(pltpu.make_async_copy(ckv_hbm.at[page], kc_buf.at[sl, dst, :],
                                             sem.at[0, sl]))
            cps.append(pltpu.make_async_copy(ckr_hbm.at[page], kp_buf.at[sl, :, dst],
                                             sem.at[1, sl]))
        return cps

    @pl.when(step == 0)
    def _():
        for cp in chunk_copies(0, 0, 0):
            cp.start()

    @pl.when(step + 1 < nb * nk)
    def _():
        nxt = step + 1
        for cp in chunk_copies(nxt // nk, nxt % nk, 1 - slot):
            cp.start()

    @pl.when(k == 0)
    def _():
        m_ref[...] = jnp.full(m_ref.shape, -jnp.inf, F32)
        l_ref[...] = jnp.zeros(l_ref.shape, F32)
        acc_ref[...] = jnp.zeros(acc_ref.shape, F32)

    q = q_ref[0].reshape(rows, QK_CAT).astype(BF16)
    qa = q[:, 0:KV_LORA]
    qp = q[:, KV_LORA:KV_LORA + ROPE_DIM]
    nt = (((1,), (1,)), ((), ()))

    def update(s, vals):
        m_prev = m_ref[...]
        m_new = jnp.maximum(m_prev, jnp.max(s, axis=-1, keepdims=True))
        alpha = jnp.exp(m_prev - m_new)
        p = jnp.exp(s - m_new)
        l_ref[...] = alpha * l_ref[...] + jnp.sum(p, axis=-1, keepdims=True)
        acc_ref[...] = alpha * acc_ref[...] + jnp.dot(p.astype(BF16), vals,
                                                      preferred_element_type=F32)
        m_ref[...] = m_new

    for cp in chunk_copies(b, k, slot):
        cp.wait()

    for i in range(CHUNK_PAGES * PAGE_SIZE // SUB_KEYS):
        keys = pl.ds(i * SUB_KEYS, SUB_KEYS)
        kc = kc_buf[slot, keys, :].astype(BF16)
        kp = kp_buf[slot, :, keys].astype(BF16)
        s = (lax.dot_general(qa, kc, nt, preferred_element_type=F32)
             + jnp.dot(qp, kp, preferred_element_type=F32))
        update(s * ATTN_SCALE, kc)

    @pl.when(k == nk - 1)
    def _():
        knew = knew_ref[0]
        knew = jnp.concatenate([knew, jnp.zeros_like(knew)], axis=0).astype(BF16)
        s = lax.dot_general(q, knew, nt, preferred_element_type=F32) * ATTN_SCALE
        tq = lax.broadcasted_iota(jnp.int32, s.shape, 0) & (ts - 1)
        tk = lax.broadcasted_iota(jnp.int32, s.shape, 1)
        s = jnp.where(tk <= tq, s, -jnp.inf)
        update(s, knew[:, 0:KV_LORA])
        o = acc_ref[...] / l_ref[...]
        o_ref[0] = o.reshape(MLA_HEADS, ts, KV_LORA).astype(o_ref.dtype)


def _attn_sample(page_table, qcat, kcat_new, cache_kv, cache_kr_t):
    b, h, ts, _ = qcat.shape
    n_pages = page_table.shape[1]
    rows = h * ts
    chunk_keys = CHUNK_PAGES * PAGE_SIZE
    grid_spec = pltpu.PrefetchScalarGridSpec(
        num_scalar_prefetch=1,
        grid=(b, n_pages // CHUNK_PAGES),
        in_specs=[pl.BlockSpec((1, h, ts, QK_CAT), lambda i, g, pt: (i, 0, 0, 0)),
                  pl.BlockSpec((1, ts, QK_CAT), lambda i, g, pt: (i, 0, 0)),
                  pl.BlockSpec(memory_space=pl.ANY),
                  pl.BlockSpec(memory_space=pl.ANY)],
        out_specs=pl.BlockSpec((1, h, ts, KV_LORA), lambda i, g, pt: (i, 0, 0, 0)),
        scratch_shapes=[pltpu.VMEM((2, chunk_keys, KV_LORA), F32),
                        pltpu.VMEM((2, ROPE_DIM, chunk_keys), F32),
                        pltpu.SemaphoreType.DMA((2, 2)),
                        pltpu.VMEM((rows, 1), F32), pltpu.VMEM((rows, 1), F32),
                        pltpu.VMEM((rows, KV_LORA), F32)],
    )
    return pl.pallas_call(
        _attn_sample_body,
        grid_spec=grid_spec,
        out_shape=jax.ShapeDtypeStruct((b, h, ts, KV_LORA), F32),
        compiler_params=_params(("arbitrary", "arbitrary")),
        name="attn_sample",
    )(page_table, qcat, kcat_new, cache_kv, cache_kr_t)


def _mla_out_body(o_ref, w_ref, out_ref):
    tb, _, tt, c = o_ref.shape
    o = o_ref[:, 0].reshape(tb * tt, c).astype(BF16)
    out_ref[...] = jnp.dot(o, w_ref[0], preferred_element_type=F32).astype(out_ref.dtype)


def _mla_out(o_lat, wuv_t, tb, tt):
    b, h, t, c = o_lat.shape
    nt = t // tt
    return pl.pallas_call(
        _mla_out_body,
        grid=(b // tb, nt, h),
        in_specs=[pl.BlockSpec((tb, 1, tt, c), lambda i, j, k: (i, k, j, 0)),
                  pl.BlockSpec((1, c, V_DIM), lambda i, j, k: (k, 0, 0))],
        out_specs=pl.BlockSpec((tb * tt, V_DIM), lambda i, j, k: (i * nt + j, k)),
        out_shape=jax.ShapeDtypeStruct((b * t, h * V_DIM), BF16),
        compiler_params=_params(("arbitrary", "arbitrary", "arbitrary")),
        name="mla_out",
    )(o_lat, wuv_t)


def _rwprep_body(x_ref, st_ref, mu_ref, w0_ref, a0_ref, wd_ref, wa_ref, wg_ref,
                 o5_ref, g_ref, sh_ref, *carry):
    ti = pl.program_id(1)
    tb, tt, c = x_ref.shape
    rows = tb * tt
    x = x_ref[...]
    last = x[:, tt - 1:tt, :]
    if carry:
        first = jnp.where(ti == 0, st_ref[...], carry[0][...])
        carry[0][...] = last
    else:
        first = st_ref[...]
    tpos = lax.broadcasted_iota(jnp.int32, x.shape, 1)
    prev = jnp.where(tpos == 0, first, pltpu.roll(x, 1, axis=1))
    sh_ref[...] = last
    mixed = x + (prev - x) * mu_ref[...]
    w = RWKV_WIDTH
    o5_ref[0] = mixed[..., 0:w]
    o5_ref[1] = mixed[..., w:2 * w]
    o5_ref[2] = mixed[..., 2 * w:3 * w]
    wd = mixed[..., 3 * w:3 * w + DECAY_LORA].reshape(rows, DECAY_LORA)
    ad = mixed[..., 3 * w + DECAY_LORA:3 * w + DECAY_LORA + ICLR_LORA].reshape(rows, ICLR_LORA)
    gd = mixed[..., 3 * w + DECAY_LORA + ICLR_LORA:c].reshape(rows, GATE_PAD)
    z = w0_ref[...] + jnp.dot(jnp.tanh(wd).astype(BF16), wd_ref[...], preferred_element_type=F32)
    w_log = -jax.nn.softplus(-z) - 0.5
    o5_ref[3] = jnp.exp(-jnp.exp(w_log)).reshape(tb, tt, w)
    a = jax.nn.sigmoid(a0_ref[...] + jnp.dot(ad.astype(BF16), wa_ref[...],
                                             preferred_element_type=F32))
    o5_ref[4] = a.reshape(tb, tt, w)
    g = jnp.dot(jax.nn.sigmoid(gd).astype(BF16), wg_ref[...], preferred_element_type=F32)
    g_ref[...] = g.reshape(tb, tt, w)


def _rwprep(rw, shift_prev, mu_p, w0, a0, wd, wa, wg, tb, tt):
    b, t, c = rw.shape
    w = RWKV_WIDTH
    full = lambda a: pl.BlockSpec(a.shape, lambda i, j: (0,) * a.ndim)
    mu_p = mu_p.reshape(1, c)
    w0 = w0.reshape(1, w)
    a0 = a0.reshape(1, w)
    return pl.pallas_call(
        _rwprep_body,
        grid=(b // tb, t // tt),
        in_specs=[pl.BlockSpec((tb, tt, c), lambda i, j: (i, j, 0)),
                  pl.BlockSpec((tb, 1, c), lambda i, j: (i, 0, 0)),
                  full(mu_p), full(w0), full(a0), full(wd), full(wa), full(wg)],
        out_specs=[pl.BlockSpec((5, tb, tt, w), lambda i, j: (0, i, j, 0)),
                   pl.BlockSpec((tb, tt, w), lambda i, j: (i, j, 0)),
                   pl.BlockSpec((tb, 1, c), lambda i, j: (i, 0, 0))],
        out_shape=[jax.ShapeDtypeStruct((5, b, t, w), F32),
                   jax.ShapeDtypeStruct((b, t, w), F32),
                   jax.ShapeDtypeStruct((b, 1, c), F32)],
        scratch_shapes=[pltpu.VMEM((tb, 1, c), F32)] if t > tt else [],
        compiler_params=_params(("arbitrary", "arbitrary")),
        name="rwkv_prep",
    )(rw, shift_prev, mu_p, w0, a0, wd, wa, wg)


def _scan_body(x_ref, s0_ref, kkp_ref, kap_ref, rkp_ref, lnw_ref, lnb_ref,
               o_ref, sfin_ref, s_ref, kk_ref, b_ref, kh_ref):
    ti = pl.program_id(1)
    tc = x_ref.shape[2]
    n = RWKV_HEAD

    @pl.when(ti == 0)
    def _():
        for i in range(n):
            s_ref[i] = s0_ref[0, pl.ds(i, n, stride=n), :]

    r = x_ref[0, 0]
    k = x_ref[1, 0]
    v = x_ref[2, 0]
    a = x_ref[4, 0]
    kk = k * kkp_ref[0]
    kk = kk / jnp.maximum(jnp.sqrt(jnp.sum(kk * kk, axis=1, keepdims=True)), 1e-12)
    kh = k * (1.0 + (a - 1.0) * kap_ref[0])
    kk_ref[...] = kk
    b_ref[...] = kk * a
    kh_ref[...] = kh

    def row(ref, t, i):
        return ref[t, pl.ds(i, 1), :]

    def step(t, s_kk):
        sa = -s_kk
        vt = x_ref[2, 0, t]
        t_next = jnp.minimum(t + 1, tc - 1)
        y = jnp.zeros((n, LANES), F32)
        s_kk_next = jnp.zeros((n, LANES), F32)
        for i in range(n):
            s_new = (s_ref[i] * x_ref[3, 0, t, pl.ds(i, 1), :]
                     + sa * row(b_ref, t, i)
                     + vt * row(kh_ref, t, i))
            s_ref[i] = s_new
            y = y + s_new * x_ref[0, 0, t, pl.ds(i, 1), :]
            s_kk_next = s_kk_next + s_new * row(kk_ref, t_next, i)
        o_ref[0, t] = y
        return s_kk_next

    s_kk0 = jnp.zeros((n, LANES), F32)
    for i in range(n):
        s_kk0 = s_kk0 + s_ref[i] * row(kk_ref, 0, i)
    lax.fori_loop(0, tc, step, s_kk0)

    y = o_ref[0]
    mu = jnp.mean(y, axis=1, keepdims=True)
    var = jnp.mean(jnp.square(y - mu), axis=1, keepdims=True)
    yn = (y - mu) * lax.rsqrt(var + GN_EPS) * lnw_ref[0] + lnb_ref[0]
    bonus = jnp.sum(r * kh * rkp_ref[0], axis=1, keepdims=True) * v
    o_ref[0] = yn + bonus

    @pl.when(ti == pl.num_programs(1) - 1)
    def _():
        for i in range(n):
            sfin_ref[0, pl.ds(i, n, stride=n), :] = s_ref[i]


def _scan(x5, s0, kkp, kap, rkp, lnw, lnb, tc):
    _, g, t, n, l = x5.shape
    par = pl.BlockSpec((1, n, l), lambda i, j: (i, 0, 0))
    return pl.pallas_call(
        _scan_body,
        grid=(g, t // tc),
        in_specs=[pl.BlockSpec((5, 1, tc, n, l), lambda i, j: (0, i, j, 0, 0)),
                  pl.BlockSpec((1, n * n, l), lambda i, j: (i, 0, 0)),
                  par, par, par, par, par],
        out_specs=[pl.BlockSpec((1, tc, n, l), lambda i, j: (i, j, 0, 0)),
                   pl.BlockSpec((1, n * n, l), lambda i, j: (i, 0, 0))],
        out_shape=[jax.ShapeDtypeStruct((g, t, n, l), F32),
                   jax.ShapeDtypeStruct((g, n * n, l), F32)],
        scratch_shapes=[pltpu.VMEM((n, n, l), F32), pltpu.VMEM((tc, n, l), F32),
                        pltpu.VMEM((tc, n, l), F32), pltpu.VMEM((tc, n, l), F32)],
        compiler_params=_params(("arbitrary", "arbitrary")),
        name="rwkv_scan",
    )(x5, s0, kkp, kap, rkp, lnw, lnb)


OUT_TK = 512


def _outproj_body(am_ref, op_ref, g_ref, w_ref, x_ref, gt_ref, gp_ref, o_ref):
    k = pl.program_id(2)
    nk = pl.num_programs(2)
    tb, tt, d = x_ref.shape

    @pl.when(k == 0)
    def _():
        o_ref[...] = jnp.zeros(o_ref.shape, F32)

    @pl.when(k < nk // 2)
    def _():
        o_ref[...] += jnp.dot(am_ref[...], w_ref[...],
                              preferred_element_type=F32).reshape(tb, tt, d)

    @pl.when(k >= nk // 2)
    def _():
        a = (op_ref[...] * g_ref[...]).astype(BF16)
        o_ref[...] += jnp.dot(a, w_ref[...], preferred_element_type=F32).reshape(tb, tt, d)

    @pl.when(k == nk - 1)
    def _():
        o_ref[...] = x_ref[...] + gt_ref[...] * _rms(o_ref[...], gp_ref[...])


def _outproj(o_mla, o_pre, g, w_out, x, mod3, gt_idx, g_post, tb, tt):
    b, t, d = x.shape
    nt = t // tt
    rows = tb * tt
    nk = d // OUT_TK
    half = nk // 2
    return pl.pallas_call(
        _outproj_body,
        grid=(b // tb, nt, nk),
        in_specs=[pl.BlockSpec((rows, OUT_TK), lambda i, j, k: (i * nt + j, jnp.minimum(k, half - 1))),
                  pl.BlockSpec((rows, OUT_TK), lambda i, j, k: (i * nt + j, jnp.maximum(k - half, 0))),
                  pl.BlockSpec((rows, OUT_TK), lambda i, j, k: (i * nt + j, jnp.maximum(k - half, 0))),
                  pl.BlockSpec((OUT_TK, d), lambda i, j, k: (k, 0)),
                  pl.BlockSpec((tb, tt, d), lambda i, j, k: (i, j, 0)),
                  pl.BlockSpec((tb, 1, d), lambda i, j, k: (i, 0, gt_idx)),
                  pl.BlockSpec((1, d), lambda i, j, k: (0, 0))],
        out_specs=pl.BlockSpec((tb, tt, d), lambda i, j, k: (i, j, 0)),
        out_shape=jax.ShapeDtypeStruct((b, t, d), F32),
        compiler_params=_params(("arbitrary", "arbitrary", "arbitrary")),
        name="out_proj",
    )(o_mla, o_pre, g, w_out, x, mod3, g_post.reshape(1, d))


FFN_TF = 256


def _ffn_in_body(h_ref, wg_ref, wu_ref, cw_ref, cb_ref, st_ref, a_ref, cs_ref, *carry, tb, tt):
    ti = pl.program_id(2)
    tf = wg_ref.shape[1]
    h = h_ref[...]
    gate = jnp.dot(h, wg_ref[...], preferred_element_type=F32).reshape(tb, tt, tf)
    up = jnp.dot(h, wu_ref[...], preferred_element_type=F32).reshape(tb, tt, tf)
    last2 = gate[:, tt - 2:tt, :]
    if carry:
        prev2 = jnp.where(ti == 0, st_ref[...], carry[0][...])
        carry[0][...] = last2
    else:
        prev2 = st_ref[...]
    p0 = prev2[:, 0:1, :]
    p1 = prev2[:, 1:2, :]
    tpos = lax.broadcasted_iota(jnp.int32, gate.shape, 1)
    g1 = jnp.where(tpos == 0, p1, pltpu.roll(gate, 1, axis=1))
    g2 = jnp.where(tpos == 0, p0, jnp.where(tpos == 1, p1, pltpu.roll(gate, 2, axis=1)))
    cs_ref[...] = last2
    cw = cw_ref[...]
    gate_c = cb_ref[...] + cw[0:1, :] * g2
    gate_c = gate_c + cw[1:2, :] * g1
    gate_c = gate_c + cw[2:3, :] * gate
    a_ref[...] = (jax.nn.gelu(gate_c) * up).reshape(tb * tt, tf).astype(BF16)


def _ffn_in(h2, w_ffn_in, conv_w, conv_b, conv_prev, b, t, tb, tt):
    d = h2.shape[1]
    nt = t // tt
    rows = tb * tt
    tf = FFN_TF
    nj = D_FF // tf
    return pl.pallas_call(
        functools.partial(_ffn_in_body, tb=tb, tt=tt),
        grid=(b // tb, nj, nt),
        in_specs=[pl.BlockSpec((rows, d), lambda i, j, k: (i * nt + k, 0)),
                  pl.BlockSpec((d, tf), lambda i, j, k: (0, j)),
                  pl.BlockSpec((d, tf), lambda i, j, k: (0, nj + j)),
                  pl.BlockSpec((CONV_W, tf), lambda i, j, k: (0, j)),
                  pl.BlockSpec((1, tf), lambda i, j, k: (0, j)),
                  pl.BlockSpec((tb, CONV_W - 1, tf), lambda i, j, k: (i, 0, j))],
        out_specs=[pl.BlockSpec((rows, tf), lambda i, j, k: (i * nt + k, j)),
                   pl.BlockSpec((tb, CONV_W - 1, tf), lambda i, j, k: (i, 0, j))],
        out_shape=[jax.ShapeDtypeStruct((b * t, D_FF), BF16),
                   jax.ShapeDtypeStruct((b, CONV_W - 1, D_FF), F32)],
        scratch_shapes=[pltpu.VMEM((tb, CONV_W - 1, tf), F32)] if nt > 1 else [],
        compiler_params=_params(("arbitrary", "arbitrary", "arbitrary")),
        name="conv_ffn_in",
    )(h2, w_ffn_in, w_ffn_in, conv_w, conv_b.reshape(1, D_FF), conv_prev)


def _final_body(x_ref, f_ref, gt_ref, gp_ref, o_ref):
    tb, tt, d = x_ref.shape
    f = f_ref[...].reshape(tb, tt, d)
    o_ref[...] = x_ref[...] + gt_ref[...] * _rms(f, gp_ref[...])


def _final(x, f, mod3, gt_idx, g_post, tb, tt):
    b, t, d = x.shape
    nt = t // tt
    return pl.pallas_call(
        _final_body,
        grid=(b // tb, nt),
        in_specs=[pl.BlockSpec((tb, tt, d), lambda i, j: (i, j, 0)),
                  pl.BlockSpec((tb * tt, d), lambda i, j: (i * nt + j, 0)),
                  pl.BlockSpec((tb, 1, d), lambda i, j: (i, 0, gt_idx)),
                  pl.BlockSpec((1, d), lambda i, j: (0, 0))],
        out_specs=pl.BlockSpec((tb, tt, d), lambda i, j: (i, j, 0)),
        out_shape=jax.ShapeDtypeStruct((b, t, d), F32),
        compiler_params=_params(("arbitrary", "arbitrary")),
        name="residual_norm",
    )(x, f, mod3, g_post.reshape(1, d))


def _rope_tables(pos):
    half = ROPE_DIM // 2
    inv = ROPE_THETA ** (-jnp.arange(half, dtype=F32) * 2.0 / ROPE_DIM)
    ang = pos.astype(F32)[:, None] * inv[None, :]
    cos, sin = jnp.cos(ang), jnp.sin(ang)
    z = jnp.zeros((pos.shape[0], LANES - ROPE_DIM), F32)
    cos_t = jnp.concatenate([cos, cos, z], axis=-1)[None]
    sin_t = jnp.concatenate([-sin, sin, z], axis=-1)[None]
    return cos_t, sin_t


def _swap_halves(w):
    half = ROPE_DIM // 2
    return jnp.concatenate([w[..., half:], w[..., :half]], axis=-1)


class _LaneLayout:
    def __init__(self, b):
        n, hh = RWKV_HEAD, RWKV_HEADS
        self.b = b
        self.by_head = b == LANES
        assert self.by_head or b * hh == LANES, b
        self.groups = hh if self.by_head else 1

    def param(self, p):
        ph = p.reshape(RWKV_HEADS, RWKV_HEAD)
        if self.by_head:
            return jnp.broadcast_to(ph[:, :, None], (RWKV_HEADS, RWKV_HEAD, LANES))
        return jnp.tile(ph.T, (1, self.b))[None]

    def tokens_in(self, x5, t):
        x = x5.reshape(5, self.b, t, RWKV_HEADS, RWKV_HEAD)
        if self.by_head:
            return x.transpose(0, 3, 2, 4, 1)
        return x.transpose(0, 2, 4, 1, 3).reshape(5, 1, t, RWKV_HEAD, LANES)

    def tokens_out(self, o, t):
        if self.by_head:
            x = o.transpose(3, 1, 0, 2)
        else:
            x = o.reshape(t, RWKV_HEAD, self.b, RWKV_HEADS).transpose(2, 0, 3, 1)
        return x.reshape(self.b * t, RWKV_WIDTH)

    def state_in(self, s):
        n = RWKV_HEAD
        if self.by_head:
            return s.transpose(1, 2, 3, 0).reshape(RWKV_HEADS, n * n, LANES)
        return s.transpose(2, 3, 0, 1).reshape(1, n * n, LANES)

    def state_out(self, s):
        n = RWKV_HEAD
        if self.by_head:
            return s.reshape(RWKV_HEADS, n, n, LANES).transpose(3, 0, 1, 2)
        return s.reshape(n, n, self.b, RWKV_HEADS).transpose(2, 3, 0, 1)


def _prep_weights(w_in, w_uq, w_uk, w_uv, mu_shift, w_decay_up, w_iclr_up, w_gate_up,
                  w_out, w_ffn_in, w_ffn_out, k_k, k_a, r_k, ln_x_w, ln_x_b):
    d = D_MODEL
    z64 = jnp.zeros((d, LANES - ROPE_DIM), F32)
    w_kr = w_in[:, Q_LORA + KV_LORA:MLA_PROJ]
    wp = {}
    wp["w_q"] = w_in[:, :Q_LORA].astype(BF16)
    wp["w_kvr"] = jnp.concatenate(
        [w_in[:, Q_LORA:Q_LORA + KV_LORA], w_kr, z64, _swap_halves(w_kr), z64], axis=1).astype(BF16)
    wp["w_rw"] = jnp.pad(w_in[:, MLA_PROJ:], ((0, 0), (0, RW_PAD - RWKV_PROJ))).astype(BF16)
    wq3 = w_uq.reshape(Q_LORA, MLA_HEADS, QK_NOPE + ROPE_DIM)
    pe = wq3[:, :, QK_NOPE:]
    padl = ((0, 0), (0, 0), (0, LANES - ROPE_DIM))
    wp["wq_all"] = jnp.concatenate(
        [wq3[:, :, :QK_NOPE].reshape(Q_LORA, -1),
         jnp.pad(pe, padl).reshape(Q_LORA, -1),
         jnp.pad(_swap_halves(pe), padl).reshape(Q_LORA, -1)], axis=1).astype(BF16)
    wp["wuk_t"] = w_uk.transpose(1, 2, 0).astype(BF16)
    wp["wuv_t"] = w_uv.transpose(1, 0, 2).astype(BF16)
    wp["mu"] = jnp.pad(mu_shift, (0, RW_PAD - RWKV_PROJ))
    wp["wd"] = w_decay_up.astype(BF16)
    wp["wa"] = w_iclr_up.astype(BF16)
    wp["wg"] = jnp.pad(w_gate_up, ((0, GATE_PAD - GATE_LORA), (0, 0))).astype(BF16)
    wp["w_out"] = w_out.astype(BF16)
    wp["w_ffn_in"] = w_ffn_in.astype(BF16)
    wp["w_ffn_out"] = w_ffn_out.astype(BF16)
    wp["head_params"] = (k_k, k_a, r_k.reshape(-1), ln_x_w, ln_x_b)
    return wp


def _run_group(x, mod3, pos, wp, prm, shift_prev, wkv_prev, conv_prev, tiles, attend):
    b, t, d = x.shape
    tb, tt = tiles["row"]
    cos, sin = _rope_tables(pos)

    h = _prenorm(x, mod3, 1, 0, prm["g_pre_mix"], tb, tt)
    tm = tiles["mm_rows"]
    q_lat = _matmul(h, wp["w_q"], tm, Q_LORA).reshape(b, t, Q_LORA)
    kvr = _matmul(h, wp["w_kvr"], tm, wp["w_kvr"].shape[1]).reshape(b, t, -1)
    rw = _matmul(h, wp["w_rw"], tm, 768).reshape(b, t, RW_PAD)

    qtb, qtt = tiles["q"]
    qcat = _qprep(q_lat, prm["g_q_latent"], wp["wq_all"], wp["wuk_t"], cos, sin, qtb, qtt,
                  tiles["cat_dtype"])
    kv_c, k_pe, kcat = _kvprep(kvr, prm["g_kv_latent"], cos, sin, tb, tt, tiles["cat_dtype"])
    o_lat = attend(qcat, kcat)
    otb, ott = tiles["o"]
    o_mla = _mla_out(o_lat, wp["wuv_t"], otb, ott)

    rtb, rtt = tiles["rw"]
    x5, g, new_shift = _rwprep(rw, shift_prev, wp["mu"], prm["w0"], prm["a0"],
                               wp["wd"], wp["wa"], wp["wg"], rtb, rtt)
    lay = _LaneLayout(b)
    o_scan, s_fin = _scan(lay.tokens_in(x5, t), lay.state_in(wkv_prev),
                          *[lay.param(p) for p in wp["head_params"]], tiles["scan_tc"])
    o_pre = lay.tokens_out(o_scan, t)
    new_wkv = lay.state_out(s_fin)

    x1 = _outproj(o_mla, o_pre, g.reshape(b * t, RWKV_WIDTH), wp["w_out"], x, mod3, 2,
                  prm["g_post_mix"], tb, tt)

    h2 = _prenorm(x1, mod3, 4, 3, prm["g_pre_ffn"], tb, tt)
    ftb, ftt = tiles["ffn"]
    act, conv_state = _ffn_in(h2, wp["w_ffn_in"], prm["conv_w"], prm["conv_b"], conv_prev,
                              b, t, ftb, ftt)
    f = _matmul(act, wp["w_ffn_out"], 512, 512)
    y = _final(x1, f, mod3, 5, prm["g_post_ffn"], tb, tt)
    return y, kv_c, k_pe, new_shift[:, 0, :RWKV_PROJ], new_wkv, conv_state


def kernel(x_prompt, x_sample, c_prompt, c_sample, cache_kv_latent, cache_k_rope, page_table,
           state_rwkv_shift, state_rwkv_wkv, state_ffn_conv, w_ada, b_ada, g_pre_mix, g_post_mix,
           g_pre_ffn, g_post_ffn, w_in, g_q_latent, w_uq, g_kv_latent, w_uk, w_uv, mu_shift, w0,
           w_decay_up, a0, w_iclr_up, w_gate_up, k_k, k_a, r_k, ln_x_w, ln_x_b, w_out, w_ffn_in,
           conv_w, conv_b, w_ffn_out):
    bp, tp, d = x_prompt.shape
    bs, ts, _ = x_sample.shape
    wp = _prep_weights(w_in, w_uq, w_uk, w_uv, mu_shift, w_decay_up, w_iclr_up, w_gate_up,
                       w_out, w_ffn_in, w_ffn_out, k_k, k_a, r_k, ln_x_w, ln_x_b)
    prm = dict(g_pre_mix=g_pre_mix, g_post_mix=g_post_mix, g_pre_ffn=g_pre_ffn,
               g_post_ffn=g_post_ffn, g_q_latent=g_q_latent, g_kv_latent=g_kv_latent,
               w0=w0, a0=a0, conv_w=conv_w, conv_b=conv_b)

    n_c = bp + bs
    n_c_pad = -(-n_c // 8) * 8
    c_all = jnp.concatenate([c_prompt, c_sample, jnp.zeros((n_c_pad - n_c, d), F32)], axis=0)
    mod = _adaln_mod(c_all, w_ada, b_ada)
    mod_p = mod[:bp].reshape(bp, 1, 6 * d)
    mod_s = mod[bp:n_c].reshape(bs, 1, 6 * d)

    tiles_p = dict(row=(1, 512), ffn=(1, 1024), mm_rows=1024, q=(1, 256), o=(1, 1024),
                   rw=(1, 128), scan_tc=16, cat_dtype=BF16)
    outs_p = _run_group(
        x_prompt, mod_p, jnp.arange(tp), wp, prm,
        jnp.zeros((bp, 1, RW_PAD), F32),
        jnp.zeros((bp, RWKV_HEADS, RWKV_HEAD, RWKV_HEAD), F32),
        jnp.zeros((bp, CONV_W - 1, D_FF), F32),
        tiles_p, _attn_prompt)

    past_len = page_table.shape[1] * cache_kv_latent.shape[1]
    tiles_s = dict(row=(32, ts), ffn=(bs, ts), mm_rows=1024, q=(16, ts), o=(bs, ts),
                   rw=(16, ts), scan_tc=ts, cat_dtype=F32)
    shift_s = jnp.pad(state_rwkv_shift, ((0, 0), (0, RW_PAD - RWKV_PROJ)))[:, None, :]
    cache_kr_t = jnp.swapaxes(cache_k_rope, 1, 2)
    attend_s = lambda qcat, kcat: _attn_sample(page_table, qcat, kcat, cache_kv_latent, cache_kr_t)
    outs_s = _run_group(
        x_sample, mod_s, past_len + jnp.arange(ts), wp, prm,
        shift_s, state_rwkv_wkv, state_ffn_conv, tiles_s, attend_s)

    y_p, kv_p, kr_p, sh_p, wkv_p, cs_p = outs_p
    y_s, kv_s, kr_s, sh_s, wkv_s, cs_s = outs_s
    return (y_p, y_s, kv_p, kr_p, sh_p, wkv_p, cs_p, kv_s, kr_s, sh_s, wkv_s, cs_s)
```

```python
import functools

import jax
import jax.numpy as jnp
from jax import lax
from jax.experimental import pallas as pl
from jax.experimental.pallas import tpu as pltpu

F32 = jnp.float32
BF16 = jnp.bfloat16

D_MODEL = 4096
MLA_HEADS = 16
QK_NOPE = 128
ROPE_DIM = 64
V_DIM = 128
Q_LORA = 896
KV_LORA = 512
ROPE_THETA = 10000.0
ATTN_SCALE = (QK_NOPE + ROPE_DIM) ** -0.5
RWKV_HEAD = 64
RWKV_WIDTH = D_MODEL - MLA_HEADS * V_DIM
RWKV_HEADS = RWKV_WIDTH // RWKV_HEAD
DECAY_LORA = 128
ICLR_LORA = 128
GATE_LORA = 480
RWKV_PROJ = 3 * RWKV_WIDTH + DECAY_LORA + ICLR_LORA + GATE_LORA
MLA_PROJ = Q_LORA + KV_LORA + ROPE_DIM
D_FF = 11008
CONV_W = 3
NORM_EPS = 1e-6
GN_EPS = 64e-5
PAGE_SIZE = 128

LANES = 128
RW_PAD = 6912
GATE_PAD = 512
QK_CAT = KV_LORA + LANES
VMEM_LIMIT = 56 * 1024 * 1024


def _params(sem):
    return pltpu.CompilerParams(dimension_semantics=sem, vmem_limit_bytes=VMEM_LIMIT)


def _rms(x, g):
    ms = jnp.mean(x * x, axis=-1, keepdims=True)
    return x * lax.rsqrt(ms + NORM_EPS) * g


def _mod_body(c_ref, w_ref, b_ref, o_ref):
    c = c_ref[...]
    s = c * jax.nn.sigmoid(c)
    o_ref[...] = jnp.dot(s.astype(BF16), w_ref[...].astype(BF16),
                         preferred_element_type=F32) + b_ref[...]


def _adaln_mod(c_all, w_ada, b_ada):
    m, d = c_all.shape
    n = w_ada.shape[1]
    tn = 512
    return pl.pallas_call(
        _mod_body,
        grid=(n // tn,),
        in_specs=[pl.BlockSpec((m, d), lambda j: (0, 0)),
                  pl.BlockSpec((d, tn), lambda j: (0, j)),
                  pl.BlockSpec((1, tn), lambda j: (0, j))],
        out_specs=pl.BlockSpec((m, tn), lambda j: (0, j)),
        out_shape=jax.ShapeDtypeStruct((m, n), F32),
        compiler_params=_params(("arbitrary",)),
        name="adaln_mod",
    )(c_all, w_ada, b_ada.reshape(1, n))


def _prenorm_body(x_ref, sc_ref, sh_ref, g_ref, o_ref):
    tb, tt, d = x_ref.shape
    y = _rms(x_ref[...], g_ref[...])
    h = y * (1.0 + sc_ref[...]) + sh_ref[...]
    o_ref[...] = h.reshape(tb * tt, d).astype(BF16)


def _prenorm(x, mod3, sc_idx, sh_idx, g, tb, tt):
    b, t, d = x.shape
    nt = t // tt
    return pl.pallas_call(
        _prenorm_body,
        grid=(b // tb, nt),
        in_specs=[pl.BlockSpec((tb, tt, d), lambda i, j: (i, j, 0)),
                  pl.BlockSpec((tb, 1, d), lambda i, j: (i, 0, sc_idx)),
                  pl.BlockSpec((tb, 1, d), lambda i, j: (i, 0, sh_idx)),
                  pl.BlockSpec((1, d), lambda i, j: (0, 0))],
        out_specs=pl.BlockSpec((tb * tt, d), lambda i, j: (i * nt + j, 0)),
        out_shape=jax.ShapeDtypeStruct((b * t, d), BF16),
        compiler_params=_params(("arbitrary", "arbitrary")),
        name="prenorm",
    )(x, mod3, mod3, g.reshape(1, d))


def _mm_body(a_ref, w_ref, o_ref):
    o_ref[...] = jnp.dot(a_ref[...], w_ref[...], preferred_element_type=F32).astype(o_ref.dtype)


def _matmul(a, w, tm, tn, out_dtype=F32):
    n, k = a.shape
    m = w.shape[1]
    return pl.pallas_call(
        _mm_body,
        grid=(n // tm, m // tn),
        in_specs=[pl.BlockSpec((tm, k), lambda i, j: (i, 0)),
                  pl.BlockSpec((k, tn), lambda i, j: (0, j))],
        out_specs=pl.BlockSpec((tm, tn), lambda i, j: (i, j)),
        out_shape=jax.ShapeDtypeStruct((n, m), out_dtype),
        compiler_params=_params(("arbitrary", "arbitrary")),
        name="matmul",
    )(a, w)


def _qprep_body(x_ref, g_ref, wq_ref, wuk_ref, cos_ref, sin_ref, o_ref):
    tb, tt, _ = x_ref.shape
    rows = tb * tt
    xn = _rms(x_ref[...], g_ref[...]).reshape(rows, Q_LORA).astype(BF16)
    q = jnp.dot(xn, wq_ref[...], preferred_element_type=F32)
    cos = cos_ref[...]
    sin = sin_ref[...]
    hw = MLA_HEADS * LANES
    for h in range(MLA_HEADS):
        qn = q[:, h * LANES:(h + 1) * LANES].astype(BF16)
        qa = jnp.dot(qn, wuk_ref[h], preferred_element_type=F32)
        o_ref[:, h, :, 0:KV_LORA] = qa.reshape(tb, tt, KV_LORA).astype(o_ref.dtype)
        pe = q[:, hw + h * LANES: hw + (h + 1) * LANES].reshape(tb, tt, LANES)
        pes = q[:, 2 * hw + h * LANES: 2 * hw + (h + 1) * LANES].reshape(tb, tt, LANES)
        o_ref[:, h, :, KV_LORA:QK_CAT] = (pe * cos + pes * sin).astype(o_ref.dtype)


def _qprep(q_lat, g_q, wq_all, wuk_t, cos, sin, tb, tt, out_dtype):
    b, t, _ = q_lat.shape
    return pl.pallas_call(
        _qprep_body,
        grid=(b // tb, t // tt),
        in_specs=[pl.BlockSpec((tb, tt, Q_LORA), lambda i, j: (i, j, 0)),
                  pl.BlockSpec((1, Q_LORA), lambda i, j: (0, 0)),
                  pl.BlockSpec(wq_all.shape, lambda i, j: (0, 0)),
                  pl.BlockSpec(wuk_t.shape, lambda i, j: (0, 0, 0)),
                  pl.BlockSpec((1, tt, LANES), lambda i, j: (0, j, 0)),
                  pl.BlockSpec((1, tt, LANES), lambda i, j: (0, j, 0))],
        out_specs=pl.BlockSpec((tb, MLA_HEADS, tt, QK_CAT), lambda i, j: (i, 0, j, 0)),
        out_shape=jax.ShapeDtypeStruct((b, MLA_HEADS, t, QK_CAT), out_dtype),
        compiler_params=_params(("arbitrary", "arbitrary")),
        name="q_prep",
    )(q_lat, g_q.reshape(1, Q_LORA), wq_all, wuk_t, cos, sin)


def _kvprep_body(x_ref, g_ref, cos_ref, sin_ref, kvc_ref, kpe_ref, kcat_ref):
    x = x_ref[...]
    kvc = _rms(x[..., 0:KV_LORA], g_ref[...])
    kpe = (x[..., KV_LORA:KV_LORA + LANES] * cos_ref[...]
           + x[..., KV_LORA + LANES:KV_LORA + 2 * LANES] * sin_ref[...])
    kvc_ref[...] = kvc
    kpe_ref[...] = kpe[..., 0:ROPE_DIM]
    kcat_ref[..., 0:KV_LORA] = kvc.astype(kcat_ref.dtype)
    kcat_ref[..., KV_LORA:QK_CAT] = kpe.astype(kcat_ref.dtype)


def _kvprep(kvr, g_kv, cos, sin, tb, tt, cat_dtype):
    b, t, w = kvr.shape
    blk = lambda n: pl.BlockSpec((tb, tt, n), lambda i, j: (i, j, 0))
    return pl.pallas_call(
        _kvprep_body,
        grid=(b // tb, t // tt),
        in_specs=[blk(w),
                  pl.BlockSpec((1, KV_LORA), lambda i, j: (0, 0)),
                  pl.BlockSpec((1, tt, LANES), lambda i, j: (0, j, 0)),
                  pl.BlockSpec((1, tt, LANES), lambda i, j: (0, j, 0))],
        out_specs=[blk(KV_LORA), blk(ROPE_DIM), blk(QK_CAT)],
        out_shape=[jax.ShapeDtypeStruct((b, t, KV_LORA), F32),
                   jax.ShapeDtypeStruct((b, t, ROPE_DIM), F32),
                   jax.ShapeDtypeStruct((b, t, QK_CAT), cat_dtype)],
        compiler_params=_params(("arbitrary", "arbitrary")),
        name="kv_prep",
    )(kvr, g_kv.reshape(1, KV_LORA), cos, sin)


ATT_TQ = 256
ATT_TK = 512
ATT_HG = 4
ATT_CHAINS = 2


def _attn_prompt_body(q_ref, k_ref, o_ref, m_ref, l_ref, acc_ref):
    qi = pl.program_id(1)
    rows = ATT_HG * ATT_TQ
    q = q_ref[0].reshape(rows, QK_CAT)
    m_ref[...] = jnp.full(m_ref.shape, -jnp.inf, F32)
    l_ref[...] = jnp.zeros(l_ref.shape, F32)
    acc_ref[...] = jnp.zeros(acc_ref.shape, F32)
    sub = rows // ATT_CHAINS
    qpos = qi * ATT_TQ + (lax.broadcasted_iota(jnp.int32, (sub, ATT_TK), 0) & (ATT_TQ - 1))
    col = lax.broadcasted_iota(jnp.int32, (sub, ATT_TK), 1)

    def body(kb, carry):
        kblk = k_ref[0, pl.ds(pl.multiple_of(kb * ATT_TK, ATT_TK), ATT_TK), :]
        vblk = kblk[:, 0:KV_LORA]
        visible = kb * ATT_TK + col <= qpos
        for c in range(ATT_CHAINS):
            rs = pl.ds(c * sub, sub)
            s = lax.dot_general(q[c * sub:(c + 1) * sub], kblk, (((1,), (1,)), ((), ())),
                                preferred_element_type=F32) * ATTN_SCALE
            s = jnp.where(visible, s, -jnp.inf)
            m_prev = m_ref[rs, :]
            m_new = jnp.maximum(m_prev, jnp.max(s, axis=-1, keepdims=True))
            alpha = jnp.exp(m_prev - m_new)
            p = jnp.exp(s - m_new)
            l_ref[rs, :] = alpha * l_ref[rs, :] + jnp.sum(p, axis=-1, keepdims=True)
            acc_ref[rs, :] = alpha * acc_ref[rs, :] + jnp.dot(
                p.astype(BF16), vblk, preferred_element_type=F32)
            m_ref[rs, :] = m_new
        return carry

    n_kb = ((qi + 1) * ATT_TQ + ATT_TK - 1) // ATT_TK
    lax.fori_loop(0, n_kb, body, 0)
    o = acc_ref[...] / l_ref[...]
    o_ref[0] = o.reshape(ATT_HG, ATT_TQ, KV_LORA).astype(o_ref.dtype)


def _attn_prompt(qcat, kcat):
    b, h, t, _ = qcat.shape
    rows = ATT_HG * ATT_TQ
    return pl.pallas_call(
        _attn_prompt_body,
        grid=(b, t // ATT_TQ, h // ATT_HG),
        in_specs=[pl.BlockSpec((1, ATT_HG, ATT_TQ, QK_CAT), lambda i, j, g: (i, g, j, 0)),
                  pl.BlockSpec((1, t, QK_CAT), lambda i, j, g: (i, 0, 0))],
        out_specs=pl.BlockSpec((1, ATT_HG, ATT_TQ, KV_LORA), lambda i, j, g: (i, g, j, 0)),
        out_shape=jax.ShapeDtypeStruct((b, h, t, KV_LORA), BF16),
        scratch_shapes=[pltpu.VMEM((rows, 1), F32), pltpu.VMEM((rows, 1), F32),
                        pltpu.VMEM((rows, KV_LORA), F32)],
        compiler_params=_params(("arbitrary", "arbitrary", "arbitrary")),
        name="attn_prompt",
    )(qcat, kcat)


CHUNK_PAGES = 32
SUB_KEYS = 1024


def _attn_sample_body(pt_ref, q_ref, knew_ref, ckv_hbm, ckr_hbm, o_ref,
                      kc_buf, kp_buf, sem, m_ref, l_ref, acc_ref):
    b = pl.program_id(0)
    k = pl.program_id(1)
    nb = pl.num_programs(0)
    nk = pl.num_programs(1)
    ts = q_ref.shape[2]
    rows = MLA_HEADS * ts
    step = b * nk + k
    slot = step & 1

    def chunk_copies(seq, chunk, sl):
        cps = []
        for j in range(CHUNK_PAGES):
            page = pt_ref[seq, chunk * CHUNK_PAGES + j]
            dst = pl.ds(j * PAGE_SIZE, PAGE_SIZE)
            cps.append(pltpu.make_async_copy(ckv_hbm.at[page], kc_buf.at[sl, dst, :],
                                             sem.at[0, sl]))
            cps.append(pltpu.make_async_copy(ckr_hbm.at[page], kp_buf.at[sl, :, dst],
                                             sem.at[1, sl]))
        return cps

    @pl.when(step == 0)
    def _():
        for cp in chunk_copies(0, 0, 0):
            cp.start()

    @pl.when(step + 1 < nb * nk)
    def _():
        nxt = step + 1
        for cp in chunk_copies(nxt // nk, nxt % nk, 1 - slot):
            cp.start()

    @pl.when(k == 0)
    def _():
        m_ref[...] = jnp.full(m_ref.shape, -jnp.inf, F32)
        l_ref[...] = jnp.zeros(l_ref.shape, F32)
        acc_ref[...] = jnp.zeros(acc_ref.shape, F32)

    q = q_ref[0].reshape(rows, QK_CAT).astype(BF16)
    qa = q[:, 0:KV_LORA]
    qp = q[:, KV_LORA:KV_LORA + ROPE_DIM]
    nt = (((1,), (1,)), ((), ()))

    def update(s_parts, val_parts):
        s = jnp.concatenate(s_parts, axis=1) if len(s_parts) > 1 else s_parts[0]
        m_prev = m_ref[...]
        m_new = jnp.maximum(m_prev, jnp.max(s, axis=-1, keepdims=True))
        alpha = jnp.exp(m_prev - m_new)
        p = jnp.exp(s - m_new)
        l_ref[...] = alpha * l_ref[...] + jnp.sum(p, axis=-1, keepdims=True)
        p = p.astype(BF16)
        pv = None
        col = 0
        for vals in val_parts:
            n = vals.shape[0]
            part = jnp.dot(p[:, col:col + n], vals, preferred_element_type=F32)
            pv = part if pv is None else pv + part
            col += n
        acc_ref[...] = alpha * acc_ref[...] + pv
        m_ref[...] = m_new

    for cp in chunk_copies(b, k, slot):
        cp.wait()

    s_parts = []
    kc_parts = []
    for i in range(CHUNK_PAGES * PAGE_SIZE // SUB_KEYS):
        keys = pl.ds(i * SUB_KEYS, SUB_KEYS)
        kc = kc_buf[slot, keys, :].astype(BF16)
        kp = kp_buf[slot, :, keys].astype(BF16)
        s = (lax.dot_general(qa, kc, nt, preferred_element_type=F32)
             + jnp.dot(qp, kp, preferred_element_type=F32))
        s_parts.append(s * ATTN_SCALE)
        kc_parts.append(kc)
    update(s_parts, kc_parts)

    @pl.when(k == nk - 1)
    def _():
        knew = knew_ref[0]
        knew = jnp.concatenate([knew, jnp.zeros_like(knew)], axis=0).astype(BF16)
        s = lax.dot_general(q, knew, nt, preferred_element_type=F32) * ATTN_SCALE
        tq = lax.broadcasted_iota(jnp.int32, s.shape, 0) & (ts - 1)
        tk = lax.broadcasted_iota(jnp.int32, s.shape, 1)
        s = jnp.where(tk <= tq, s, -jnp.inf)
        update([s], [knew[:, 0:KV_LORA]])
        o = acc_ref[...] / l_ref[...]
        o_ref[0] = o.reshape(MLA_HEADS, ts, KV_LORA).astype(o_ref.dtype)


def _attn_sample(page_table, qcat, kcat_new, cache_kv, cache_kr_t):
    b, h, ts, _ = qcat.shape
    n_pages = page_table.shape[1]
    rows = h * ts
    chunk_keys = CHUNK_PAGES * PAGE_SIZE
    grid_spec = pltpu.PrefetchScalarGridSpec(
        num_scalar_prefetch=1,
        grid=(b, n_pages // CHUNK_PAGES),
        in_specs=[pl.BlockSpec((1, h, ts, QK_CAT), lambda i, g, pt: (i, 0, 0, 0)),
                  pl.BlockSpec((1, ts, QK_CAT), lambda i, g, pt: (i, 0, 0)),
                  pl.BlockSpec(memory_space=pl.ANY),
                  pl.BlockSpec(memory_space=pl.ANY)],
        out_specs=pl.BlockSpec((1, h, ts, KV_LORA), lambda i, g, pt: (i, 0, 0, 0)),
        scratch_shapes=[pltpu.VMEM((2, chunk_keys, KV_LORA), F32),
                        pltpu.VMEM((2, ROPE_DIM, chunk_keys), F32),
                        pltpu.SemaphoreType.DMA((2, 2)),
                        pltpu.VMEM((rows, 1), F32), pltpu.VMEM((rows, 1), F32),
                        pltpu.VMEM((rows, KV_LORA), F32)],
    )
    return pl.pallas_call(
        _attn_sample_body,
        grid_spec=grid_spec,
        out_shape=jax.ShapeDtypeStruct((b, h, ts, KV_LORA), F32),
        compiler_params=_params(("arbitrary", "arbitrary")),
        name="attn_sample",
    )(page_table, qcat, kcat_new, cache_kv, cache_kr_t)


def _mla_out_body(o_ref, w_ref, out_ref):
    tb, _, tt, c = o_ref.shape
    o = o_ref[:, 0].reshape(tb * tt, c).astype(BF16)
    out_ref[...] = jnp.dot(o, w_ref[0], preferred_element_type=F32).astype(out_ref.dtype)


def _mla_out(o_lat, wuv_t, tb, tt):
    b, h, t, c = o_lat.shape
    nt = t // tt
    return pl.pallas_call(
        _mla_out_body,
        grid=(b // tb, nt, h),
        in_specs=[pl.BlockSpec((tb, 1, tt, c), lambda i, j, k: (i, k, j, 0)),
                  pl.BlockSpec((1, c, V_DIM), lambda i, j, k: (k, 0, 0))],
        out_specs=pl.BlockSpec((tb * tt, V_DIM), lambda i, j, k: (i * nt + j, k)),
        out_shape=jax.ShapeDtypeStruct((b * t, h * V_DIM), BF16),
        compiler_params=_params(("arbitrary", "arbitrary", "arbitrary")),
        name="mla_out",
    )(o_lat, wuv_t)


def _rwprep_body(x_ref, st_ref, mu_ref, w0_ref, a0_ref, wd_ref, wa_ref, wg_ref,
                 o5_ref, g_ref, sh_ref, *carry):
    ti = pl.program_id(1)
    tb, tt, c = x_ref.shape
    rows = tb * tt
    x = x_ref[...]
    last = x[:, tt - 1:tt, :]
    if carry:
        first = jnp.where(ti == 0, st_ref[...], carry[0][...])
        carry[0][...] = last
    else:
        first = st_ref[...]
    tpos = lax.broadcasted_iota(jnp.int32, x.shape, 1)
    prev = jnp.where(tpos == 0, first, pltpu.roll(x, 1, axis=1))
    sh_ref[...] = last
    mixed = x + (prev - x) * mu_ref[...]
    w = RWKV_WIDTH
    o5_ref[0] = mixed[..., 0:w]
    o5_ref[1] = mixed[..., w:2 * w]
    o5_ref[2] = mixed[..., 2 * w:3 * w]
    wd = mixed[..., 3 * w:3 * w + DECAY_LORA].reshape(rows, DECAY_LORA)
    ad = mixed[..., 3 * w + DECAY_LORA:3 * w + DECAY_LORA + ICLR_LORA].reshape(rows, ICLR_LORA)
    gd = mixed[..., 3 * w + DECAY_LORA + ICLR_LORA:c].reshape(rows, GATE_PAD)
    z = w0_ref[...] + jnp.dot(jnp.tanh(wd).astype(BF16), wd_ref[...], preferred_element_type=F32)
    w_log = -jax.nn.softplus(-z) - 0.5
    o5_ref[3] = jnp.exp(-jnp.exp(w_log)).reshape(tb, tt, w)
    a = jax.nn.sigmoid(a0_ref[...] + jnp.dot(ad.astype(BF16), wa_ref[...],
                                             preferred_element_type=F32))
    o5_ref[4] = a.reshape(tb, tt, w)
    g = jnp.dot(jax.nn.sigmoid(gd).astype(BF16), wg_ref[...], preferred_element_type=F32)
    g_ref[...] = g.reshape(tb, tt, w)


def _rwprep(rw, shift_prev, mu_p, w0, a0, wd, wa, wg, tb, tt):
    b, t, c = rw.shape
    w = RWKV_WIDTH
    full = lambda a: pl.BlockSpec(a.shape, lambda i, j: (0,) * a.ndim)
    mu_p = mu_p.reshape(1, c)
    w0 = w0.reshape(1, w)
    a0 = a0.reshape(1, w)
    return pl.pallas_call(
        _rwprep_body,
        grid=(b // tb, t // tt),
        in_specs=[pl.BlockSpec((tb, tt, c), lambda i, j: (i, j, 0)),
                  pl.BlockSpec((tb, 1, c), lambda i, j: (i, 0, 0)),
                  full(mu_p), full(w0), full(a0), full(wd), full(wa), full(wg)],
        out_specs=[pl.BlockSpec((5, tb, tt, w), lambda i, j: (0, i, j, 0)),
                   pl.BlockSpec((tb, tt, w), lambda i, j: (i, j, 0)),
                   pl.BlockSpec((tb, 1, c), lambda i, j: (i, 0, 0))],
        out_shape=[jax.ShapeDtypeStruct((5, b, t, w), F32),
                   jax.ShapeDtypeStruct((b, t, w), F32),
                   jax.ShapeDtypeStruct((b, 1, c), F32)],
        scratch_shapes=[pltpu.VMEM((tb, 1, c), F32)] if t > tt else [],
        compiler_params=_params(("arbitrary", "arbitrary")),
        name="rwkv_prep",
    )(rw, shift_prev, mu_p, w0, a0, wd, wa, wg)


def _scan_body(x_ref, s0_ref, kkp_ref, kap_ref, rkp_ref, lnw_ref, lnb_ref,
               o_ref, sfin_ref, s_ref, kk_ref, b_ref, kh_ref):
    ti = pl.program_id(1)
    tc = x_ref.shape[2]
    n = RWKV_HEAD

    @pl.when(ti == 0)
    def _():
        for i in range(n):
            s_ref[i] = s0_ref[0, pl.ds(i, n, stride=n), :]

    r = x_ref[0, 0]
    k = x_ref[1, 0]
    v = x_ref[2, 0]
    a = x_ref[4, 0]
    kk = k * kkp_ref[0]
    kk = kk / jnp.maximum(jnp.sqrt(jnp.sum(kk * kk, axis=1, keepdims=True)), 1e-12)
    kh = k * (1.0 + (a - 1.0) * kap_ref[0])
    kk_ref[...] = kk
    b_ref[...] = kk * a
    kh_ref[...] = kh

    def row(ref, t, i):
        return ref[t, pl.ds(i, 1), :]

    def step(t, s_kk):
        sa = -s_kk
        vt = x_ref[2, 0, t]
        t_next = jnp.minimum(t + 1, tc - 1)
        y = jnp.zeros((n, LANES), F32)
        s_kk_next = jnp.zeros((n, LANES), F32)
        for i in range(n):
            s_new = (s_ref[i] * x_ref[3, 0, t, pl.ds(i, 1), :]
                     + sa * row(b_ref, t, i)
                     + vt * row(kh_ref, t, i))
            s_ref[i] = s_new
            y = y + s_new * x_ref[0, 0, t, pl.ds(i, 1), :]
            s_kk_next = s_kk_next + s_new * row(kk_ref, t_next, i)
        o_ref[0, t] = y
        return s_kk_next

    s_kk0 = jnp.zeros((n, LANES), F32)
    for i in range(n):
        s_kk0 = s_kk0 + s_ref[i] * row(kk_ref, 0, i)
    lax.fori_loop(0, tc, step, s_kk0)

    y = o_ref[0]
    mu = jnp.mean(y, axis=1, keepdims=True)
    var = jnp.mean(jnp.square(y - mu), axis=1, keepdims=True)
    yn = (y - mu) * lax.rsqrt(var + GN_EPS) * lnw_ref[0] + lnb_ref[0]
    bonus = jnp.sum(r * kh * rkp_ref[0], axis=1, keepdims=True) * v
    o_ref[0] = yn + bonus

    @pl.when(ti == pl.num_programs(1) - 1)
    def _():
        for i in range(n):
            sfin_ref[0, pl.ds(i, n, stride=n), :] = s_ref[i]


def _scan(x5, s0, kkp, kap, rkp, lnw, lnb, tc):
    _, g, t, n, l = x5.shape
    par = pl.BlockSpec((1, n, l), lambda i, j: (i, 0, 0))
    return pl.pallas_call(
        _scan_body,
        grid=(g, t // tc),
        in_specs=[pl.BlockSpec((5, 1, tc, n, l), lambda i, j: (0, i, j, 0, 0)),
                  pl.BlockSpec((1, n * n, l), lambda i, j: (i, 0, 0)),
                  par, par, par, par, par],
        out_specs=[pl.BlockSpec((1, tc, n, l), lambda i, j: (i, j, 0, 0)),
                   pl.BlockSpec((1, n * n, l), lambda i, j: (i, 0, 0))],
        out_shape=[jax.ShapeDtypeStruct((g, t, n, l), F32),
                   jax.ShapeDtypeStruct((g, n * n, l), F32)],
        scratch_shapes=[pltpu.VMEM((n, n, l), F32), pltpu.VMEM((tc, n, l), F32),
                        pltpu.VMEM((tc, n, l), F32), pltpu.VMEM((tc, n, l), F32)],
        compiler_params=_params(("arbitrary", "arbitrary")),
        name="rwkv_scan",
    )(x5, s0, kkp, kap, rkp, lnw, lnb)


OUT_TK = 512


def _outproj_body(am_ref, op_ref, g_ref, w_ref, x_ref, gt_ref, gp_ref, o_ref):
    k = pl.program_id(2)
    nk = pl.num_programs(2)
    tb, tt, d = x_ref.shape

    @pl.when(k == 0)
    def _():
        o_ref[...] = jnp.zeros(o_ref.shape, F32)

    @pl.when(k < nk // 2)
    def _():
        o_ref[...] += jnp.dot(am_ref[...], w_ref[...],
                              preferred_element_type=F32).reshape(tb, tt, d)

    @pl.when(k >= nk // 2)
    def _():
        a = (op_ref[...] * g_ref[...]).astype(BF16)
        o_ref[...] += jnp.dot(a, w_ref[...], preferred_element_type=F32).reshape(tb, tt, d)

    @pl.when(k == nk - 1)
    def _():
        o_ref[...] = x_ref[...] + gt_ref[...] * _rms(o_ref[...], gp_ref[...])


def _outproj(o_mla, o_pre, g, w_out, x, mod3, gt_idx, g_post, tb, tt):
    b, t, d = x.shape
    nt = t // tt
    rows = tb * tt
    nk = d // OUT_TK
    half = nk // 2
    return pl.pallas_call(
        _outproj_body,
        grid=(b // tb, nt, nk),
        in_specs=[pl.BlockSpec((rows, OUT_TK), lambda i, j, k: (i * nt + j, jnp.minimum(k, half - 1))),
                  pl.BlockSpec((rows, OUT_TK), lambda i, j, k: (i * nt + j, jnp.maximum(k - half, 0))),
                  pl.BlockSpec((rows, OUT_TK), lambda i, j, k: (i * nt + j, jnp.maximum(k - half, 0))),
                  pl.BlockSpec((OUT_TK, d), lambda i, j, k: (k, 0)),
                  pl.BlockSpec((tb, tt, d), lambda i, j, k: (i, j, 0)),
                  pl.BlockSpec((tb, 1, d), lambda i, j, k: (i, 0, gt_idx)),
                  pl.BlockSpec((1, d), lambda i, j, k: (0, 0))],
        out_specs=pl.BlockSpec((tb, tt, d), lambda i, j, k: (i, j, 0)),
        out_shape=jax.ShapeDtypeStruct((b, t, d), F32),
        compiler_params=_params(("arbitrary", "arbitrary", "arbitrary")),
        name="out_proj",
    )(o_mla, o_pre, g, w_out, x, mod3, g_post.reshape(1, d))


FFN_TF = 256


def _ffn_in_body(h_ref, wg_ref, wu_ref, cw_ref, cb_ref, st_ref, a_ref, cs_ref, wg_bf, wu_bf,
                 *carry, tb, tt):
    ti = pl.program_id(2)
    tf = wg_ref.shape[1]

    @pl.when((pl.program_id(1) == 0) & (ti == 0))
    def _():
        wg_bf[...] = wg_ref[...].astype(BF16)
        wu_bf[...] = wu_ref[...].astype(BF16)

    h = h_ref[...]
    gate = jnp.dot(h, wg_bf[...], preferred_element_type=F32).reshape(tb, tt, tf)
    up = jnp.dot(h, wu_bf[...], preferred_element_type=F32).reshape(tb, tt, tf)
    last2 = gate[:, tt - 2:tt, :]
    if carry:
        prev2 = jnp.where(ti == 0, st_ref[...], carry[0][...])
        carry[0][...] = last2
    else:
        prev2 = st_ref[...]
    p0 = prev2[:, 0:1, :]
    p1 = prev2[:, 1:2, :]
    tpos = lax.broadcasted_iota(jnp.int32, gate.shape, 1)
    g1 = jnp.where(tpos == 0, p1, pltpu.roll(gate, 1, axis=1))
    g2 = jnp.where(tpos == 0, p0, jnp.where(tpos == 1, p1, pltpu.roll(gate, 2, axis=1)))
    cs_ref[...] = last2
    cw = cw_ref[...]
    gate_c = cb_ref[...] + cw[0:1, :] * g2
    gate_c = gate_c + cw[1:2, :] * g1
    gate_c = gate_c + cw[2:3, :] * gate
    a_ref[...] = (jax.nn.gelu(gate_c) * up).reshape(tb * tt, tf).astype(BF16)


def _ffn_in(h2, w_ffn_in, conv_w, conv_b, conv_prev, b, t, tb, tt):
    d = h2.shape[1]
    nt = t // tt
    rows = tb * tt
    tf = FFN_TF
    nj = D_FF // tf
    return pl.pallas_call(
        functools.partial(_ffn_in_body, tb=tb, tt=tt),
        grid=(nj, b // tb, nt),
        in_specs=[pl.BlockSpec((rows, d), lambda j, i, k: (i * nt + k, 0)),
                  pl.BlockSpec((d, tf), lambda j, i, k: (0, j)),
                  pl.BlockSpec((d, tf), lambda j, i, k: (0, nj + j)),
                  pl.BlockSpec((CONV_W, tf), lambda j, i, k: (0, j)),
                  pl.BlockSpec((1, tf), lambda j, i, k: (0, j)),
                  pl.BlockSpec((tb, CONV_W - 1, tf), lambda j, i, k: (i, 0, j))],
        out_specs=[pl.BlockSpec((rows, tf), lambda j, i, k: (i * nt + k, j)),
                   pl.BlockSpec((tb, CONV_W - 1, tf), lambda j, i, k: (i, 0, j))],
        out_shape=[jax.ShapeDtypeStruct((b * t, D_FF), BF16),
                   jax.ShapeDtypeStruct((b, CONV_W - 1, D_FF), F32)],
        scratch_shapes=([pltpu.VMEM((d, tf), BF16), pltpu.VMEM((d, tf), BF16)]
                        + ([pltpu.VMEM((tb, CONV_W - 1, tf), F32)] if nt > 1 else [])),
        compiler_params=_params(("arbitrary", "arbitrary", "arbitrary")),
        name="conv_ffn_in",
    )(h2, w_ffn_in, w_ffn_in, conv_w, conv_b.reshape(1, D_FF), conv_prev)


def _final_body(x_ref, f_ref, gt_ref, gp_ref, o_ref):
    tb, tt, d = x_ref.shape
    f = f_ref[...].reshape(tb, tt, d)
    o_ref[...] = x_ref[...] + gt_ref[...] * _rms(f, gp_ref[...])


def _final(x, f, mod3, gt_idx, g_post, tb, tt):
    b, t, d = x.shape
    nt = t // tt
    return pl.pallas_call(
        _final_body,
        grid=(b // tb, nt),
        in_specs=[pl.BlockSpec((tb, tt, d), lambda i, j: (i, j, 0)),
                  pl.BlockSpec((tb * tt, d), lambda i, j: (i * nt + j, 0)),
                  pl.BlockSpec((tb, 1, d), lambda i, j: (i, 0, gt_idx)),
                  pl.BlockSpec((1, d), lambda i, j: (0, 0))],
        out_specs=pl.BlockSpec((tb, tt, d), lambda i, j: (i, j, 0)),
        out_shape=jax.ShapeDtypeStruct((b, t, d), F32),
        compiler_params=_params(("arbitrary", "arbitrary")),
        name="residual_norm",
    )(x, f, mod3, g_post.reshape(1, d))


def _rope_tables(pos):
    half = ROPE_DIM // 2
    inv = ROPE_THETA ** (-jnp.arange(half, dtype=F32) * 2.0 / ROPE_DIM)
    ang = pos.astype(F32)[:, None] * inv[None, :]
    cos, sin = jnp.cos(ang), jnp.sin(ang)
    z = jnp.zeros((pos.shape[0], LANES - ROPE_DIM), F32)
    cos_t = jnp.concatenate([cos, cos, z], axis=-1)[None]
    sin_t = jnp.concatenate([-sin, sin, z], axis=-1)[None]
    return cos_t, sin_t


def _swap_halves(w):
    half = ROPE_DIM // 2
    return jnp.concatenate([w[..., half:], w[..., :half]], axis=-1)


class _LaneLayout:
    def __init__(self, b):
        n, hh = RWKV_HEAD, RWKV_HEADS
        self.b = b
        self.by_head = b == LANES
        assert self.by_head or b * hh == LANES, b
        self.groups = hh if self.by_head else 1

    def param(self, p):
        ph = p.reshape(RWKV_HEADS, RWKV_HEAD)
        if self.by_head:
            return jnp.broadcast_to(ph[:, :, None], (RWKV_HEADS, RWKV_HEAD, LANES))
        return jnp.tile(ph.T, (1, self.b))[None]

    def tokens_in(self, x5, t):
        x = x5.reshape(5, self.b, t, RWKV_HEADS, RWKV_HEAD)
        if self.by_head:
            return x.transpose(0, 3, 2, 4, 1)
        return x.transpose(0, 2, 4, 1, 3).reshape(5, 1, t, RWKV_HEAD, LANES)

    def tokens_out(self, o, t):
        if self.by_head:
            x = o.transpose(3, 1, 0, 2)
        else:
            x = o.reshape(t, RWKV_HEAD, self.b, RWKV_HEADS).transpose(2, 0, 3, 1)
        return x.reshape(self.b * t, RWKV_WIDTH)

    def state_in(self, s):
        n = RWKV_HEAD
        if self.by_head:
            return s.transpose(1, 2, 3, 0).reshape(RWKV_HEADS, n * n, LANES)
        return s.transpose(2, 3, 0, 1).reshape(1, n * n, LANES)

    def state_out(self, s):
        n = RWKV_HEAD
        if self.by_head:
            return s.reshape(RWKV_HEADS, n, n, LANES).transpose(3, 0, 1, 2)
        return s.reshape(n, n, self.b, RWKV_HEADS).transpose(2, 3, 0, 1)


def _prep_weights(w_in, w_uq, w_uk, w_uv, mu_shift, w_decay_up, w_iclr_up, w_gate_up,
                  w_out, w_ffn_in, w_ffn_out, k_k, k_a, r_k, ln_x_w, ln_x_b):
    d = D_MODEL
    z64 = jnp.zeros((d, LANES - ROPE_DIM), F32)
    w_kr = w_in[:, Q_LORA + KV_LORA:MLA_PROJ]
    wp = {}
    wp["w_q"] = w_in[:, :Q_LORA].astype(BF16)
    wp["w_kvr"] = jnp.concatenate(
        [w_in[:, Q_LORA:Q_LORA + KV_LORA], w_kr, z64, _swap_halves(w_kr), z64], axis=1).astype(BF16)
    wp["w_rw"] = jnp.pad(w_in[:, MLA_PROJ:], ((0, 0), (0, RW_PAD - RWKV_PROJ))).astype(BF16)
    wq3 = w_uq.reshape(Q_LORA, MLA_HEADS, QK_NOPE + ROPE_DIM)
    pe = wq3[:, :, QK_NOPE:]
    padl = ((0, 0), (0, 0), (0, LANES - ROPE_DIM))
    wp["wq_all"] = jnp.concatenate(
        [wq3[:, :, :QK_NOPE].reshape(Q_LORA, -1),
         jnp.pad(pe, padl).reshape(Q_LORA, -1),
         jnp.pad(_swap_halves(pe), padl).reshape(Q_LORA, -1)], axis=1).astype(BF16)
    wp["wuk_t"] = w_uk.transpose(1, 2, 0).astype(BF16)
    wp["wuv_t"] = w_uv.transpose(1, 0, 2).astype(BF16)
    wp["mu"] = jnp.pad(mu_shift, (0, RW_PAD - RWKV_PROJ))
    wp["wd"] = w_decay_up.astype(BF16)
    wp["wa"] = w_iclr_up.astype(BF16)
    wp["wg"] = jnp.pad(w_gate_up, ((0, GATE_PAD - GATE_LORA), (0, 0))).astype(BF16)
    wp["w_out"] = w_out.astype(BF16)
    wp["w_ffn_in"] = w_ffn_in
    wp["w_ffn_out"] = w_ffn_out.astype(BF16)
    wp["head_params"] = (k_k, k_a, r_k.reshape(-1), ln_x_w, ln_x_b)
    return wp


def _run_group(x, mod3, pos, wp, prm, shift_prev, wkv_prev, conv_prev, tiles, attend):
    b, t, d = x.shape
    tb, tt = tiles["row"]
    cos, sin = _rope_tables(pos)

    h = _prenorm(x, mod3, 1, 0, prm["g_pre_mix"], tb, tt)
    tm = tiles["mm_rows"]
    q_lat = _matmul(h, wp["w_q"], tm, Q_LORA).reshape(b, t, Q_LORA)
    kvr = _matmul(h, wp["w_kvr"], tm, wp["w_kvr"].shape[1]).reshape(b, t, -1)
    rw = _matmul(h, wp["w_rw"], tm, 768).reshape(b, t, RW_PAD)

    qtb, qtt = tiles["q"]
    qcat = _qprep(q_lat, prm["g_q_latent"], wp["wq_all"], wp["wuk_t"], cos, sin, qtb, qtt,
                  tiles["cat_dtype"])
    kv_c, k_pe, kcat = _kvprep(kvr, prm["g_kv_latent"], cos, sin, tb, tt, tiles["cat_dtype"])
    o_lat = attend(qcat, kcat)
    otb, ott = tiles["o"]
    o_mla = _mla_out(o_lat, wp["wuv_t"], otb, ott)

    rtb, rtt = tiles["rw"]
    x5, g, new_shift = _rwprep(rw, shift_prev, wp["mu"], prm["w0"], prm["a0"],
                               wp["wd"], wp["wa"], wp["wg"], rtb, rtt)
    lay = _LaneLayout(b)
    o_scan, s_fin = _scan(lay.tokens_in(x5, t), lay.state_in(wkv_prev),
                          *[lay.param(p) for p in wp["head_params"]], tiles["scan_tc"])
    o_pre = lay.tokens_out(o_scan, t)
    new_wkv = lay.state_out(s_fin)

    x1 = _outproj(o_mla, o_pre, g.reshape(b * t, RWKV_WIDTH), wp["w_out"], x, mod3, 2,
                  prm["g_post_mix"], tb, tt)

    h2 = _prenorm(x1, mod3, 4, 3, prm["g_pre_ffn"], tb, tt)
    ftb, ftt = tiles["ffn"]
    act, conv_state = _ffn_in(h2, wp["w_ffn_in"], prm["conv_w"], prm["conv_b"], conv_prev,
                              b, t, ftb, ftt)
    f = _matmul(act, wp["w_ffn_out"], 512, 512)
    y = _final(x1, f, mod3, 5, prm["g_post_ffn"], tb, tt)
    return y, kv_c, k_pe, new_shift[:, 0, :RWKV_PROJ], new_wkv, conv_state


def kernel(x_prompt, x_sample, c_prompt, c_sample, cache_kv_latent, cache_k_rope, page_table,
           state_rwkv_shift, state_rwkv_wkv, state_ffn_conv, w_ada, b_ada, g_pre_mix, g_post_mix,
           g_pre_ffn, g_post_ffn, w_in, g_q_latent, w_uq, g_kv_latent, w_uk, w_uv, mu_shift, w0,
           w_decay_up, a0, w_iclr_up, w_gate_up, k_k, k_a, r_k, ln_x_w, ln_x_b, w_out, w_ffn_in,
           conv_w, conv_b, w_ffn_out):
    bp, tp, d = x_prompt.shape
    bs, ts, _ = x_sample.shape
    wp = _prep_weights(w_in, w_uq, w_uk, w_uv, mu_shift, w_decay_up, w_iclr_up, w_gate_up,
                       w_out, w_ffn_in, w_ffn_out, k_k, k_a, r_k, ln_x_w, ln_x_b)
    prm = dict(g_pre_mix=g_pre_mix, g_post_mix=g_post_mix, g_pre_ffn=g_pre_ffn,
               g_post_ffn=g_post_ffn, g_q_latent=g_q_latent, g_kv_latent=g_kv_latent,
               w0=w0, a0=a0, conv_w=conv_w, conv_b=conv_b)

    n_c = bp + bs
    n_c_pad = -(-n_c // 8) * 8
    c_all = jnp.concatenate([c_prompt, c_sample, jnp.zeros((n_c_pad - n_c, d), F32)], axis=0)
    mod = _adaln_mod(c_all, w_ada, b_ada)
    mod_p = mod[:bp].reshape(bp, 1, 6 * d)
    mod_s = mod[bp:n_c].reshape(bs, 1, 6 * d)

    tiles_p = dict(row=(1, 512), ffn=(1, 1024), mm_rows=1024, q=(1, 256), o=(1, 1024),
                   rw=(1, 128), scan_tc=16, cat_dtype=BF16)
    outs_p = _run_group(
        x_prompt, mod_p, jnp.arange(tp), wp, prm,
        jnp.zeros((bp, 1, RW_PAD), F32),
        jnp.zeros((bp, RWKV_HEADS, RWKV_HEAD, RWKV_HEAD), F32),
        jnp.zeros((bp, CONV_W - 1, D_FF), F32),
        tiles_p, _attn_prompt)

    past_len = page_table.shape[1] * cache_kv_latent.shape[1]
    tiles_s = dict(row=(32, ts), ffn=(bs, ts), mm_rows=1024, q=(16, ts), o=(bs, ts),
                   rw=(16, ts), scan_tc=ts, cat_dtype=F32)
    shift_s = jnp.pad(state_rwkv_shift, ((0, 0), (0, RW_PAD - RWKV_PROJ)))[:, None, :]
    cache_kr_t = jnp.swapaxes(cache_k_rope, 1, 2)
    attend_s = lambda qcat, kcat: _attn_sample(page_table, qcat, kcat, cache_kv_latent, cache_kr_t)
    outs_s = _run_group(
        x_sample, mod_s, past_len + jnp.arange(ts), wp, prm,
        shift_s, state_rwkv_wkv, state_ffn_conv, tiles_s, attend_s)

    y_p, kv_p, kr_p, sh_p, wkv_p, cs_p = outs_p
    y_s, kv_s, kr_s, sh_s, wkv_s, cs_s = outs_s
    return (y_p, y_s, kv_p, kr_p, sh_p, wkv_p, cs_p, kv_s, kr_s, sh_s, wkv_s, cs_s)
```

```python
import functools

import jax
import jax.numpy as jnp
from jax import lax
from jax.experimental import pallas as pl
from jax.experimental.pallas import tpu as pltpu

F32 = jnp.float32
BF16 = jnp.bfloat16

D_MODEL = 4096
MLA_HEADS = 16
QK_NOPE = 128
ROPE_DIM = 64
V_DIM = 128
Q_LORA = 896
KV_LORA = 512
ROPE_THETA = 10000.0
ATTN_SCALE = (QK_NOPE + ROPE_DIM) ** -0.5
RWKV_HEAD = 64
RWKV_WIDTH = D_MODEL - MLA_HEADS * V_DIM
RWKV_HEADS = RWKV_WIDTH // RWKV_HEAD
DECAY_LORA = 128
ICLR_LORA = 128
GATE_LORA = 480
RWKV_PROJ = 3 * RWKV_WIDTH + DECAY_LORA + ICLR_LORA + GATE_LORA
MLA_PROJ = Q_LORA + KV_LORA + ROPE_DIM
D_FF = 11008
CONV_W = 3
NORM_EPS = 1e-6
GN_EPS = 64e-5
PAGE_SIZE = 128

LANES = 128
RW_PAD = 6912
GATE_PAD = 512
QK_CAT = KV_LORA + LANES
VMEM_LIMIT = 56 * 1024 * 1024


def _params(sem):
    return pltpu.CompilerParams(dimension_semantics=sem, vmem_limit_bytes=VMEM_LIMIT)


def _rms(x, g):
    ms = jnp.mean(x * x, axis=-1, keepdims=True)
    return x * lax.rsqrt(ms + NORM_EPS) * g


def _mod_body(c_ref, w_ref, b_ref, o_ref):
    c = c_ref[...]
    s = c * jax.nn.sigmoid(c)
    o_ref[...] = jnp.dot(s.astype(BF16), w_ref[...].astype(BF16),
                         preferred_element_type=F32) + b_ref[...]


def _adaln_mod(c_all, w_ada, b_ada):
    m, d = c_all.shape
    n = w_ada.shape[1]
    tn = 512
    return pl.pallas_call(
        _mod_body,
        grid=(n // tn,),
        in_specs=[pl.BlockSpec((m, d), lambda j: (0, 0)),
                  pl.BlockSpec((d, tn), lambda j: (0, j)),
                  pl.BlockSpec((1, tn), lambda j: (0, j))],
        out_specs=pl.BlockSpec((m, tn), lambda j: (0, j)),
        out_shape=jax.ShapeDtypeStruct((m, n), F32),
        compiler_params=_params(("arbitrary",)),
        name="adaln_mod",
    )(c_all, w_ada, b_ada.reshape(1, n))


def _prenorm_body(x_ref, sc_ref, sh_ref, g_ref, o_ref):
    tb, tt, d = x_ref.shape
    y = _rms(x_ref[...], g_ref[...])
    h = y * (1.0 + sc_ref[...]) + sh_ref[...]
    o_ref[...] = h.reshape(tb * tt, d).astype(BF16)


def _prenorm(x, mod3, sc_idx, sh_idx, g, tb, tt):
    b, t, d = x.shape
    nt = t // tt
    return pl.pallas_call(
        _prenorm_body,
        grid=(b // tb, nt),
        in_specs=[pl.BlockSpec((tb, tt, d), lambda i, j: (i, j, 0)),
                  pl.BlockSpec((tb, 1, d), lambda i, j: (i, 0, sc_idx)),
                  pl.BlockSpec((tb, 1, d), lambda i, j: (i, 0, sh_idx)),
                  pl.BlockSpec((1, d), lambda i, j: (0, 0))],
        out_specs=pl.BlockSpec((tb * tt, d), lambda i, j: (i * nt + j, 0)),
        out_shape=jax.ShapeDtypeStruct((b * t, d), BF16),
        compiler_params=_params(("arbitrary", "arbitrary")),
        name="prenorm",
    )(x, mod3, mod3, g.reshape(1, d))


def _mm_body(a_ref, w_ref, o_ref):
    o_ref[...] = jnp.dot(a_ref[...], w_ref[...], preferred_element_type=F32).astype(o_ref.dtype)


def _matmul(a, w, tm, tn, out_dtype=F32):
    n, k = a.shape
    m = w.shape[1]
    return pl.pallas_call(
        _mm_body,
        grid=(n // tm, m // tn),
        in_specs=[pl.BlockSpec((tm, k), lambda i, j: (i, 0)),
                  pl.BlockSpec((k, tn), lambda i, j: (0, j))],
        out_specs=pl.BlockSpec((tm, tn), lambda i, j: (i, j)),
        out_shape=jax.ShapeDtypeStruct((n, m), out_dtype),
        compiler_params=_params(("arbitrary", "arbitrary")),
        name="matmul",
    )(a, w)


def _qprep_body(x_ref, g_ref, wq_ref, wuk_ref, cos_ref, sin_ref, o_ref):
    tb, tt, _ = x_ref.shape
    rows = tb * tt
    xn = _rms(x_ref[...], g_ref[...]).reshape(rows, Q_LORA).astype(BF16)
    q = jnp.dot(xn, wq_ref[...], preferred_element_type=F32)
    cos = cos_ref[...]
    sin = sin_ref[...]
    hw = MLA_HEADS * LANES
    for h in range(MLA_HEADS):
        qn = q[:, h * LANES:(h + 1) * LANES].astype(BF16)
        qa = jnp.dot(qn, wuk_ref[h], preferred_element_type=F32)
        o_ref[:, h, :, 0:KV_LORA] = qa.reshape(tb, tt, KV_LORA).astype(o_ref.dtype)
        pe = q[:, hw + h * LANES: hw + (h + 1) * LANES].reshape(tb, tt, LANES)
        pes = q[:, 2 * hw + h * LANES: 2 * hw + (h + 1) * LANES].reshape(tb, tt, LANES)
        o_ref[:, h, :, KV_LORA:QK_CAT] = (pe * cos + pes * sin).astype(o_ref.dtype)


def _qprep(q_lat, g_q, wq_all, wuk_t, cos, sin, tb, tt, out_dtype):
    b, t, _ = q_lat.shape
    return pl.pallas_call(
        _qprep_body,
        grid=(b // tb, t // tt),
        in_specs=[pl.BlockSpec((tb, tt, Q_LORA), lambda i, j: (i, j, 0)),
                  pl.BlockSpec((1, Q_LORA), lambda i, j: (0, 0)),
                  pl.BlockSpec(wq_all.shape, lambda i, j: (0, 0)),
                  pl.BlockSpec(wuk_t.shape, lambda i, j: (0, 0, 0)),
                  pl.BlockSpec((1, tt, LANES), lambda i, j: (0, j, 0)),
                  pl.BlockSpec((1, tt, LANES), lambda i, j: (0, j, 0))],
        out_specs=pl.BlockSpec((tb, MLA_HEADS, tt, QK_CAT), lambda i, j: (i, 0, j, 0)),
        out_shape=jax.ShapeDtypeStruct((b, MLA_HEADS, t, QK_CAT), out_dtype),
        compiler_params=_params(("arbitrary", "arbitrary")),
        name="q_prep",
    )(q_lat, g_q.reshape(1, Q_LORA), wq_all, wuk_t, cos, sin)


def _kvprep_body(x_ref, g_ref, cos_ref, sin_ref, kvc_ref, kpe_ref, kcat_ref):
    x = x_ref[...]
    kvc = _rms(x[..., 0:KV_LORA], g_ref[...])
    kpe = (x[..., KV_LORA:KV_LORA + LANES] * cos_ref[...]
           + x[..., KV_LORA + LANES:KV_LORA + 2 * LANES] * sin_ref[...])
    kvc_ref[...] = kvc
    kpe_ref[...] = kpe[..., 0:ROPE_DIM]
    kcat_ref[..., 0:KV_LORA] = kvc.astype(kcat_ref.dtype)
    kcat_ref[..., KV_LORA:QK_CAT] = kpe.astype(kcat_ref.dtype)


def _kvprep(kvr, g_kv, cos, sin, tb, tt, cat_dtype):
    b, t, w = kvr.shape
    blk = lambda n: pl.BlockSpec((tb, tt, n), lambda i, j: (i, j, 0))
    return pl.pallas_call(
        _kvprep_body,
        grid=(b // tb, t // tt),
        in_specs=[blk(w),
                  pl.BlockSpec((1, KV_LORA), lambda i, j: (0, 0)),
                  pl.BlockSpec((1, tt, LANES), lambda i, j: (0, j, 0)),
                  pl.BlockSpec((1, tt, LANES), lambda i, j: (0, j, 0))],
        out_specs=[blk(KV_LORA), blk(ROPE_DIM), blk(QK_CAT)],
        out_shape=[jax.ShapeDtypeStruct((b, t, KV_LORA), F32),
                   jax.ShapeDtypeStruct((b, t, ROPE_DIM), F32),
                   jax.ShapeDtypeStruct((b, t, QK_CAT), cat_dtype)],
        compiler_params=_params(("arbitrary", "arbitrary")),
        name="kv_prep",
    )(kvr, g_kv.reshape(1, KV_LORA), cos, sin)


ATT_TQ = 256
ATT_TK = 512
ATT_HG = 8
ATT_CHAINS = 8


def _attn_prompt_body(q_ref, k_ref, o_ref, m_ref, l_ref, acc_ref):
    qi = pl.program_id(1)
    rows = ATT_HG * ATT_TQ
    q = q_ref[0].reshape(rows, QK_CAT)
    m_ref[...] = jnp.full(m_ref.shape, -jnp.inf, F32)
    l_ref[...] = jnp.zeros(l_ref.shape, F32)
    acc_ref[...] = jnp.zeros(acc_ref.shape, F32)
    sub = rows // ATT_CHAINS
    qpos = qi * ATT_TQ + (lax.broadcasted_iota(jnp.int32, (sub, ATT_TK), 0) & (ATT_TQ - 1))
    col = lax.broadcasted_iota(jnp.int32, (sub, ATT_TK), 1)

    def body(kb, carry):
        kblk = k_ref[0, pl.ds(pl.multiple_of(kb * ATT_TK, ATT_TK), ATT_TK), :]
        vblk = kblk[:, 0:KV_LORA]
        visible = kb * ATT_TK + col <= qpos
        scores = [lax.dot_general(q[c * sub:(c + 1) * sub], kblk, (((1,), (1,)), ((), ())),
                                  preferred_element_type=F32) for c in range(ATT_CHAINS)]
        for c in range(ATT_CHAINS):
            rs = pl.ds(c * sub, sub)
            s = jnp.where(visible, scores[c] * ATTN_SCALE, -jnp.inf)
            m_prev = m_ref[rs, :]
            m_new = jnp.maximum(m_prev, jnp.max(s, axis=-1, keepdims=True))
            alpha = jnp.exp(m_prev - m_new)
            p = jnp.exp(s - m_new)
            l_ref[rs, :] = alpha * l_ref[rs, :] + jnp.sum(p, axis=-1, keepdims=True)
            acc_ref[rs, :] = alpha * acc_ref[rs, :] + jnp.dot(
                p.astype(BF16), vblk, preferred_element_type=F32)
            m_ref[rs, :] = m_new
        return carry

    n_kb = ((qi + 1) * ATT_TQ + ATT_TK - 1) // ATT_TK
    lax.fori_loop(0, n_kb, body, 0)
    o = acc_ref[...] / l_ref[...]
    o_ref[0] = o.reshape(ATT_HG, ATT_TQ, KV_LORA).astype(o_ref.dtype)


def _attn_prompt(qcat, kcat):
    b, h, t, _ = qcat.shape
    rows = ATT_HG * ATT_TQ
    return pl.pallas_call(
        _attn_prompt_body,
        grid=(b, t // ATT_TQ, h // ATT_HG),
        in_specs=[pl.BlockSpec((1, ATT_HG, ATT_TQ, QK_CAT), lambda i, j, g: (i, g, j, 0)),
                  pl.BlockSpec((1, t, QK_CAT), lambda i, j, g: (i, 0, 0))],
        out_specs=pl.BlockSpec((1, ATT_HG, ATT_TQ, KV_LORA), lambda i, j, g: (i, g, j, 0)),
        out_shape=jax.ShapeDtypeStruct((b, h, t, KV_LORA), BF16),
        scratch_shapes=[pltpu.VMEM((rows, 1), F32), pltpu.VMEM((rows, 1), F32),
                        pltpu.VMEM((rows, KV_LORA), F32)],
        compiler_params=_params(("arbitrary", "arbitrary", "arbitrary")),
        name="attn_prompt",
    )(qcat, kcat)


CHUNK_PAGES = 32
SUB_KEYS = 1024


def _attn_sample_body(pt_ref, q_ref, knew_ref, ckv_hbm, ckr_hbm, o_ref,
                      kc_buf, kp_buf, sem, m_ref, l_ref, acc_ref):
    b = pl.program_id(0)
    k = pl.program_id(1)
    nb = pl.num_programs(0)
    nk = pl.num_programs(1)
    ts = q_ref.shape[2]
    rows = MLA_HEADS * ts
    step = b * nk + k
    slot = step & 1

    def chunk_copies(seq, chunk, sl):
        cps = []
        for j in range(CHUNK_PAGES):
            page = pt_ref[seq, chunk * CHUNK_PAGES + j]
            dst = pl.ds(j * PAGE_SIZE, PAGE_SIZE)
            cps.append(pltpu.make_async_copy(ckv_hbm.at[page], kc_buf.at[sl, dst, :],
                                             sem.at[0, sl]))
            cps.append(pltpu.make_async_copy(ckr_hbm.at[page], kp_buf.at[sl, :, dst],
                                             sem.at[1, sl]))
        return cps

    @pl.when(step == 0)
    def _():
        for cp in chunk_copies(0, 0, 0):
            cp.start()

    @pl.when(step + 1 < nb * nk)
    def _():
        nxt = step + 1
        for cp in chunk_copies(nxt // nk, nxt % nk, 1 - slot):
            cp.start()

    @pl.when(k == 0)
    def _():
        m_ref[...] = jnp.full(m_ref.shape, -jnp.inf, F32)
        l_ref[...] = jnp.zeros(l_ref.shape, F32)
        acc_ref[...] = jnp.zeros(acc_ref.shape, F32)

    q = q_ref[0].reshape(rows, QK_CAT).astype(BF16)
    qa = q[:, 0:KV_LORA]
    qp = q[:, KV_LORA:KV_LORA + ROPE_DIM]
    nt = (((1,), (1,)), ((), ()))

    def update(s_parts, val_parts):
        s = jnp.concatenate(s_parts, axis=1) if len(s_parts) > 1 else s_parts[0]
        m_prev = m_ref[...]
        m_new = jnp.maximum(m_prev, jnp.max(s, axis=-1, keepdims=True))
        alpha = jnp.exp(m_prev - m_new)
        p = jnp.exp(s - m_new)
        l_ref[...] = alpha * l_ref[...] + jnp.sum(p, axis=-1, keepdims=True)
        p = p.astype(BF16)
        pv = None
        col = 0
        for vals in val_parts:
            n = vals.shape[0]
            part = jnp.dot(p[:, col:col + n], vals, preferred_element_type=F32)
            pv = part if pv is None else pv + part
            col += n
        acc_ref[...] = alpha * acc_ref[...] + pv
        m_ref[...] = m_new

    for cp in chunk_copies(b, k, slot):
        cp.wait()

    s_parts = []
    kc_parts = []
    for i in range(CHUNK_PAGES * PAGE_SIZE // SUB_KEYS):
        keys = pl.ds(i * SUB_KEYS, SUB_KEYS)
        kc = kc_buf[slot, keys, :].astype(BF16)
        kp = kp_buf[slot, :, keys].astype(BF16)
        s = (lax.dot_general(qa, kc, nt, preferred_element_type=F32)
             + jnp.dot(qp, kp, preferred_element_type=F32))
        s_parts.append(s * ATTN_SCALE)
        kc_parts.append(kc)
    half = len(s_parts) // 2
    update(s_parts[:half], kc_parts[:half])
    update(s_parts[half:], kc_parts[half:])

    @pl.when(k == nk - 1)
    def _():
        knew = knew_ref[0]
        knew = jnp.concatenate([knew, jnp.zeros_like(knew)], axis=0).astype(BF16)
        s = lax.dot_general(q, knew, nt, preferred_element_type=F32) * ATTN_SCALE
        tq = lax.broadcasted_iota(jnp.int32, s.shape, 0) & (ts - 1)
        tk = lax.broadcasted_iota(jnp.int32, s.shape, 1)
        s = jnp.where(tk <= tq, s, -jnp.inf)
        update([s], [knew[:, 0:KV_LORA]])
        o = acc_ref[...] / l_ref[...]
        o_ref[0] = o.reshape(MLA_HEADS, ts, KV_LORA).astype(o_ref.dtype)


def _attn_sample(page_table, qcat, kcat_new, cache_kv, cache_kr_t):
    b, h, ts, _ = qcat.shape
    n_pages = page_table.shape[1]
    rows = h * ts
    chunk_keys = CHUNK_PAGES * PAGE_SIZE
    grid_spec = pltpu.PrefetchScalarGridSpec(
        num_scalar_prefetch=1,
        grid=(b, n_pages // CHUNK_PAGES),
        in_specs=[pl.BlockSpec((1, h, ts, QK_CAT), lambda i, g, pt: (i, 0, 0, 0)),
                  pl.BlockSpec((1, ts, QK_CAT), lambda i, g, pt: (i, 0, 0)),
                  pl.BlockSpec(memory_space=pl.ANY),
                  pl.BlockSpec(memory_space=pl.ANY)],
        out_specs=pl.BlockSpec((1, h, ts, KV_LORA), lambda i, g, pt: (i, 0, 0, 0)),
        scratch_shapes=[pltpu.VMEM((2, chunk_keys, KV_LORA), F32),
                        pltpu.VMEM((2, ROPE_DIM, chunk_keys), F32),
                        pltpu.SemaphoreType.DMA((2, 2)),
                        pltpu.VMEM((rows, 1), F32), pltpu.VMEM((rows, 1), F32),
                        pltpu.VMEM((rows, KV_LORA), F32)],
    )
    return pl.pallas_call(
        _attn_sample_body,
        grid_spec=grid_spec,
        out_shape=jax.ShapeDtypeStruct((b, h, ts, KV_LORA), F32),
        compiler_params=_params(("arbitrary", "arbitrary")),
        name="attn_sample",
    )(page_table, qcat, kcat_new, cache_kv, cache_kr_t)


def _mla_out_body(o_ref, w_ref, out_ref):
    tb, nh, tt, c = o_ref.shape
    for h in range(nh):
        o = o_ref[:, h].reshape(tb * tt, c).astype(BF16)
        out_ref[:, h * V_DIM:(h + 1) * V_DIM] = jnp.dot(
            o, w_ref[h], preferred_element_type=F32).astype(out_ref.dtype)


def _mla_out(o_lat, wuv_t, tb, tt):
    b, h, t, c = o_lat.shape
    nt = t // tt
    return pl.pallas_call(
        _mla_out_body,
        grid=(b // tb, nt),
        in_specs=[pl.BlockSpec((tb, h, tt, c), lambda i, j: (i, 0, j, 0)),
                  pl.BlockSpec((h, c, V_DIM), lambda i, j: (0, 0, 0))],
        out_specs=pl.BlockSpec((tb * tt, h * V_DIM), lambda i, j: (i * nt + j, 0)),
        out_shape=jax.ShapeDtypeStruct((b * t, h * V_DIM), BF16),
        compiler_params=_params(("arbitrary", "arbitrary")),
        name="mla_out",
    )(o_lat, wuv_t)


def _rwprep_body(x_ref, st_ref, mu_ref, w0_ref, a0_ref, wd_ref, wa_ref, wg_ref,
                 o5_ref, g_ref, sh_ref, *carry):
    ti = pl.program_id(1)
    tb, tt, c = x_ref.shape
    rows = tb * tt
    x = x_ref[...]
    last = x[:, tt - 1:tt, :]
    if carry:
        first = jnp.where(ti == 0, st_ref[...], carry[0][...])
        carry[0][...] = last
    else:
        first = st_ref[...]
    tpos = lax.broadcasted_iota(jnp.int32, x.shape, 1)
    prev = jnp.where(tpos == 0, first, pltpu.roll(x, 1, axis=1))
    sh_ref[...] = last
    mixed = x + (prev - x) * mu_ref[...]
    w = RWKV_WIDTH
    o5_ref[0] = mixed[..., 0:w]
    o5_ref[1] = mixed[..., w:2 * w]
    o5_ref[2] = mixed[..., 2 * w:3 * w]
    wd = mixed[..., 3 * w:3 * w + DECAY_LORA].reshape(rows, DECAY_LORA)
    ad = mixed[..., 3 * w + DECAY_LORA:3 * w + DECAY_LORA + ICLR_LORA].reshape(rows, ICLR_LORA)
    gd = mixed[..., 3 * w + DECAY_LORA + ICLR_LORA:c].reshape(rows, GATE_PAD)
    z = w0_ref[...] + jnp.dot(jnp.tanh(wd).astype(BF16), wd_ref[...], preferred_element_type=F32)
    w_log = -jax.nn.softplus(-z) - 0.5
    o5_ref[3] = jnp.exp(-jnp.exp(w_log)).reshape(tb, tt, w)
    a = jax.nn.sigmoid(a0_ref[...] + jnp.dot(ad.astype(BF16), wa_ref[...],
                                             preferred_element_type=F32))
    o5_ref[4] = a.reshape(tb, tt, w)
    g = jnp.dot(jax.nn.sigmoid(gd).astype(BF16), wg_ref[...], preferred_element_type=F32)
    g_ref[...] = g.reshape(tb, tt, w)


def _rwprep(rw, shift_prev, mu_p, w0, a0, wd, wa, wg, tb, tt):
    b, t, c = rw.shape
    w = RWKV_WIDTH
    full = lambda a: pl.BlockSpec(a.shape, lambda i, j: (0,) * a.ndim)
    mu_p = mu_p.reshape(1, c)
    w0 = w0.reshape(1, w)
    a0 = a0.reshape(1, w)
    return pl.pallas_call(
        _rwprep_body,
        grid=(b // tb, t // tt),
        in_specs=[pl.BlockSpec((tb, tt, c), lambda i, j: (i, j, 0)),
                  pl.BlockSpec((tb, 1, c), lambda i, j: (i, 0, 0)),
                  full(mu_p), full(w0), full(a0), full(wd), full(wa), full(wg)],
        out_specs=[pl.BlockSpec((5, tb, tt, w), lambda i, j: (0, i, j, 0)),
                   pl.BlockSpec((tb, tt, w), lambda i, j: (i, j, 0)),
                   pl.BlockSpec((tb, 1, c), lambda i, j: (i, 0, 0))],
        out_shape=[jax.ShapeDtypeStruct((5, b, t, w), F32),
                   jax.ShapeDtypeStruct((b, t, w), F32),
                   jax.ShapeDtypeStruct((b, 1, c), F32)],
        scratch_shapes=[pltpu.VMEM((tb, 1, c), F32)] if t > tt else [],
        compiler_params=_params(("arbitrary", "arbitrary")),
        name="rwkv_prep",
    )(rw, shift_prev, mu_p, w0, a0, wd, wa, wg)


def _scan_body(x_ref, s0_ref, kkp_ref, kap_ref, rkp_ref, lnw_ref, lnb_ref,
               o_ref, sfin_ref, s_ref, kk_ref, b_ref, kh_ref):
    ti = pl.program_id(1)
    tc = x_ref.shape[2]
    n = RWKV_HEAD

    @pl.when(ti == 0)
    def _():
        for i in range(n):
            s_ref[i] = s0_ref[0, pl.ds(i, n, stride=n), :]

    r = x_ref[0, 0]
    k = x_ref[1, 0]
    v = x_ref[2, 0]
    a = x_ref[4, 0]
    kk = k * kkp_ref[0]
    kk = kk / jnp.maximum(jnp.sqrt(jnp.sum(kk * kk, axis=1, keepdims=True)), 1e-12)
    kh = k * (1.0 + (a - 1.0) * kap_ref[0])
    kk_ref[...] = kk
    b_ref[...] = kk * a
    kh_ref[...] = kh

    def row(ref, t, i):
        return ref[t, pl.ds(i, 1), :]

    def step(t, s_kk):
        sa = -s_kk
        vt = x_ref[2, 0, t]
        t_next = jnp.minimum(t + 1, tc - 1)
        y = jnp.zeros((n, LANES), F32)
        s_kk_next = jnp.zeros((n, LANES), F32)
        for i in range(n):
            s_new = (s_ref[i] * x_ref[3, 0, t, pl.ds(i, 1), :]
                     + sa * row(b_ref, t, i)
                     + vt * row(kh_ref, t, i))
            s_ref[i] = s_new
            y = y + s_new * x_ref[0, 0, t, pl.ds(i, 1), :]
            s_kk_next = s_kk_next + s_new * row(kk_ref, t_next, i)
        o_ref[0, t] = y
        return s_kk_next

    s_kk0 = jnp.zeros((n, LANES), F32)
    for i in range(n):
        s_kk0 = s_kk0 + s_ref[i] * row(kk_ref, 0, i)
    lax.fori_loop(0, tc, step, s_kk0)

    y = o_ref[0]
    mu = jnp.mean(y, axis=1, keepdims=True)
    var = jnp.mean(jnp.square(y - mu), axis=1, keepdims=True)
    yn = (y - mu) * lax.rsqrt(var + GN_EPS) * lnw_ref[0] + lnb_ref[0]
    bonus = jnp.sum(r * kh * rkp_ref[0], axis=1, keepdims=True) * v
    o_ref[0] = yn + bonus

    @pl.when(ti == pl.num_programs(1) - 1)
    def _():
        for i in range(n):
            sfin_ref[0, pl.ds(i, n, stride=n), :] = s_ref[i]


def _scan(x5, s0, kkp, kap, rkp, lnw, lnb, tc):
    _, g, t, n, l = x5.shape
    par = pl.BlockSpec((1, n, l), lambda i, j: (i, 0, 0))
    return pl.pallas_call(
        _scan_body,
        grid=(g, t // tc),
        in_specs=[pl.BlockSpec((5, 1, tc, n, l), lambda i, j: (0, i, j, 0, 0)),
                  pl.BlockSpec((1, n * n, l), lambda i, j: (i, 0, 0)),
                  par, par, par, par, par],
        out_specs=[pl.BlockSpec((1, tc, n, l), lambda i, j: (i, j, 0, 0)),
                   pl.BlockSpec((1, n * n, l), lambda i, j: (i, 0, 0))],
        out_shape=[jax.ShapeDtypeStruct((g, t, n, l), F32),
                   jax.ShapeDtypeStruct((g, n * n, l), F32)],
        scratch_shapes=[pltpu.VMEM((n, n, l), F32), pltpu.VMEM((tc, n, l), F32),
                        pltpu.VMEM((tc, n, l), F32), pltpu.VMEM((tc, n, l), F32)],
        compiler_params=_params(("arbitrary", "arbitrary")),
        name="rwkv_scan",
    )(x5, s0, kkp, kap, rkp, lnw, lnb)


OUT_TK = 512


def _outproj_body(am_ref, op_ref, g_ref, w_ref, x_ref, gt_ref, gp_ref, o_ref):
    k = pl.program_id(2)
    nk = pl.num_programs(2)
    tb, tt, d = x_ref.shape

    @pl.when(k == 0)
    def _():
        o_ref[...] = jnp.zeros(o_ref.shape, F32)

    @pl.when(k < nk // 2)
    def _():
        o_ref[...] += jnp.dot(am_ref[...], w_ref[...],
                              preferred_element_type=F32).reshape(tb, tt, d)

    @pl.when(k >= nk // 2)
    def _():
        a = (op_ref[...] * g_ref[...]).astype(BF16)
        o_ref[...] += jnp.dot(a, w_ref[...], preferred_element_type=F32).reshape(tb, tt, d)

    @pl.when(k == nk - 1)
    def _():
        o_ref[...] = x_ref[...] + gt_ref[...] * _rms(o_ref[...], gp_ref[...])


def _outproj(o_mla, o_pre, g, w_out, x, mod3, gt_idx, g_post, tb, tt):
    b, t, d = x.shape
    nt = t // tt
    rows = tb * tt
    nk = d // OUT_TK
    half = nk // 2
    return pl.pallas_call(
        _outproj_body,
        grid=(b // tb, nt, nk),
        in_specs=[pl.BlockSpec((rows, OUT_TK), lambda i, j, k: (i * nt + j, jnp.minimum(k, half - 1))),
                  pl.BlockSpec((rows, OUT_TK), lambda i, j, k: (i * nt + j, jnp.maximum(k - half, 0))),
                  pl.BlockSpec((rows, OUT_TK), lambda i, j, k: (i * nt + j, jnp.maximum(k - half, 0))),
                  pl.BlockSpec((OUT_TK, d), lambda i, j, k: (k, 0)),
                  pl.BlockSpec((tb, tt, d), lambda i, j, k: (i, j, 0)),
                  pl.BlockSpec((tb, 1, d), lambda i, j, k: (i, 0, gt_idx)),
                  pl.BlockSpec((1, d), lambda i, j, k: (0, 0))],
        out_specs=pl.BlockSpec((tb, tt, d), lambda i, j, k: (i, j, 0)),
        out_shape=jax.ShapeDtypeStruct((b, t, d), F32),
        compiler_params=_params(("arbitrary", "arbitrary", "arbitrary")),
        name="out_proj",
    )(o_mla, o_pre, g, w_out, x, mod3, g_post.reshape(1, d))


FFN_TF = 256


def _ffn_in_body(h_ref, wg_ref, wu_ref, cw_ref, cb_ref, st_ref, a_ref, cs_ref, wg_bf, wu_bf,
                 *carry, tb, tt):
    ti = pl.program_id(2)
    tf = wg_ref.shape[1]

    @pl.when((pl.program_id(1) == 0) & (ti == 0))
    def _():
        wg_bf[...] = wg_ref[...].astype(BF16)
        wu_bf[...] = wu_ref[...].astype(BF16)

    h = h_ref[...]
    gate = jnp.dot(h, wg_bf[...], preferred_element_type=F32).reshape(tb, tt, tf)
    up = jnp.dot(h, wu_bf[...], preferred_element_type=F32).reshape(tb, tt, tf)
    last2 = gate[:, tt - 2:tt, :]
    if carry:
        prev2 = jnp.where(ti == 0, st_ref[...], carry[0][...])
        carry[0][...] = last2
    else:
        prev2 = st_ref[...]
    p0 = prev2[:, 0:1, :]
    p1 = prev2[:, 1:2, :]
    tpos = lax.broadcasted_iota(jnp.int32, gate.shape, 1)
    g1 = jnp.where(tpos == 0, p1, pltpu.roll(gate, 1, axis=1))
    g2 = jnp.where(tpos == 0, p0, jnp.where(tpos == 1, p1, pltpu.roll(gate, 2, axis=1)))
    cs_ref[...] = last2
    cw = cw_ref[...]
    gate_c = cb_ref[...] + cw[0:1, :] * g2
    gate_c = gate_c + cw[1:2, :] * g1
    gate_c = gate_c + cw[2:3, :] * gate
    a_ref[...] = (jax.nn.gelu(gate_c) * up).reshape(tb * tt, tf).astype(BF16)


def _ffn_in(h2, w_ffn_in, conv_w, conv_b, conv_prev, b, t, tb, tt):
    d = h2.shape[1]
    nt = t // tt
    rows = tb * tt
    tf = FFN_TF
    nj = D_FF // tf
    return pl.pallas_call(
        functools.partial(_ffn_in_body, tb=tb, tt=tt),
        grid=(nj, b // tb, nt),
        in_specs=[pl.BlockSpec((rows, d), lambda j, i, k: (i * nt + k, 0)),
                  pl.BlockSpec((d, tf), lambda j, i, k: (0, j)),
                  pl.BlockSpec((d, tf), lambda j, i, k: (0, nj + j)),
                  pl.BlockSpec((CONV_W, tf), lambda j, i, k: (0, j)),
                  pl.BlockSpec((1, tf), lambda j, i, k: (0, j)),
                  pl.BlockSpec((tb, CONV_W - 1, tf), lambda j, i, k: (i, 0, j))],
        out_specs=[pl.BlockSpec((rows, tf), lambda j, i, k: (i * nt + k, j)),
                   pl.BlockSpec((tb, CONV_W - 1, tf), lambda j, i, k: (i, 0, j))],
        out_shape=[jax.ShapeDtypeStruct((b * t, D_FF), BF16),
                   jax.ShapeDtypeStruct((b, CONV_W - 1, D_FF), F32)],
        scratch_shapes=([pltpu.VMEM((d, tf), BF16), pltpu.VMEM((d, tf), BF16)]
                        + ([pltpu.VMEM((tb, CONV_W - 1, tf), F32)] if nt > 1 else [])),
        compiler_params=_params(("arbitrary", "arbitrary", "arbitrary")),
        name="conv_ffn_in",
    )(h2, w_ffn_in, w_ffn_in, conv_w, conv_b.reshape(1, D_FF), conv_prev)


def _final_body(x_ref, f_ref, gt_ref, gp_ref, o_ref):
    tb, tt, d = x_ref.shape
    f = f_ref[...].reshape(tb, tt, d)
    o_ref[...] = x_ref[...] + gt_ref[...] * _rms(f, gp_ref[...])


def _final(x, f, mod3, gt_idx, g_post, tb, tt):
    b, t, d = x.shape
    nt = t // tt
    return pl.pallas_call(
        _final_body,
        grid=(b // tb, nt),
        in_specs=[pl.BlockSpec((tb, tt, d), lambda i, j: (i, j, 0)),
                  pl.BlockSpec((tb * tt, d), lambda i, j: (i * nt + j, 0)),
                  pl.BlockSpec((tb, 1, d), lambda i, j: (i, 0, gt_idx)),
                  pl.BlockSpec((1, d), lambda i, j: (0, 0))],
        out_specs=pl.BlockSpec((tb, tt, d), lambda i, j: (i, j, 0)),
        out_shape=jax.ShapeDtypeStruct((b, t, d), F32),
        compiler_params=_params(("arbitrary", "arbitrary")),
        name="residual_norm",
    )(x, f, mod3, g_post.reshape(1, d))


def _rope_tables(pos):
    half = ROPE_DIM // 2
    inv = ROPE_THETA ** (-jnp.arange(half, dtype=F32) * 2.0 / ROPE_DIM)
    ang = pos.astype(F32)[:, None] * inv[None, :]
    cos, sin = jnp.cos(ang), jnp.sin(ang)
    z = jnp.zeros((pos.shape[0], LANES - ROPE_DIM), F32)
    cos_t = jnp.concatenate([cos, cos, z], axis=-1)[None]
    sin_t = jnp.concatenate([-sin, sin, z], axis=-1)[None]
    return cos_t, sin_t


def _swap_halves(w):
    half = ROPE_DIM // 2
    return jnp.concatenate([w[..., half:], w[..., :half]], axis=-1)


class _LaneLayout:
    def __init__(self, b):
        n, hh = RWKV_HEAD, RWKV_HEADS
        self.b = b
        self.by_head = b == LANES
        assert self.by_head or b * hh == LANES, b
        self.groups = hh if self.by_head else 1

    def param(self, p):
        ph = p.reshape(RWKV_HEADS, RWKV_HEAD)
        if self.by_head:
            return jnp.broadcast_to(ph[:, :, None], (RWKV_HEADS, RWKV_HEAD, LANES))
        return jnp.tile(ph.T, (1, self.b))[None]

    def tokens_in(self, x5, t):
        x = x5.reshape(5, self.b, t, RWKV_HEADS, RWKV_HEAD)
        if self.by_head:
            return x.transpose(0, 3, 2, 4, 1)
        return x.transpose(0, 2, 4, 1, 3).reshape(5, 1, t, RWKV_HEAD, LANES)

    def tokens_out(self, o, t):
        if self.by_head:
            x = o.transpose(3, 1, 0, 2)
        else:
            x = o.reshape(t, RWKV_HEAD, self.b, RWKV_HEADS).transpose(2, 0, 3, 1)
        return x.reshape(self.b * t, RWKV_WIDTH)

    def state_in(self, s):
        n = RWKV_HEAD
        if self.by_head:
            return s.transpose(1, 2, 3, 0).reshape(RWKV_HEADS, n * n, LANES)
        return s.transpose(2, 3, 0, 1).reshape(1, n * n, LANES)

    def state_out(self, s):
        n = RWKV_HEAD
        if self.by_head:
            return s.reshape(RWKV_HEADS, n, n, LANES).transpose(3, 0, 1, 2)
        return s.reshape(n, n, self.b, RWKV_HEADS).transpose(2, 3, 0, 1)


def _prep_weights(w_in, w_uq, w_uk, w_uv, mu_shift, w_decay_up, w_iclr_up, w_gate_up,
                  w_out, w_ffn_in, w_ffn_out, k_k, k_a, r_k, ln_x_w, ln_x_b):
    d = D_MODEL
    z64 = jnp.zeros((d, LANES - ROPE_DIM), F32)
    w_kr = w_in[:, Q_LORA + KV_LORA:MLA_PROJ]
    wp = {}
    wp["w_q"] = w_in[:, :Q_LORA].astype(BF16)
    wp["w_kvr"] = jnp.concatenate(
        [w_in[:, Q_LORA:Q_LORA + KV_LORA], w_kr, z64, _swap_halves(w_kr), z64], axis=1).astype(BF16)
    wp["w_rw"] = jnp.pad(w_in[:, MLA_PROJ:], ((0, 0), (0, RW_PAD - RWKV_PROJ))).astype(BF16)
    wq3 = w_uq.reshape(Q_LORA, MLA_HEADS, QK_NOPE + ROPE_DIM)
    pe = wq3[:, :, QK_NOPE:]
    padl = ((0, 0), (0, 0), (0, LANES - ROPE_DIM))
    wp["wq_all"] = jnp.concatenate(
        [wq3[:, :, :QK_NOPE].reshape(Q_LORA, -1),
         jnp.pad(pe, padl).reshape(Q_LORA, -1),
         jnp.pad(_swap_halves(pe), padl).reshape(Q_LORA, -1)], axis=1).astype(BF16)
    wp["wuk_t"] = w_uk.transpose(1, 2, 0).astype(BF16)
    wp["wuv_t"] = w_uv.transpose(1, 0, 2).astype(BF16)
    wp["mu"] = jnp.pad(mu_shift, (0, RW_PAD - RWKV_PROJ))
    wp["wd"] = w_decay_up.astype(BF16)
    wp["wa"] = w_iclr_up.astype(BF16)
    wp["wg"] = jnp.pad(w_gate_up, ((0, GATE_PAD - GATE_LORA), (0, 0))).astype(BF16)
    wp["w_out"] = w_out.astype(BF16)
    wp["w_ffn_in"] = w_ffn_in
    wp["w_ffn_out"] = w_ffn_out.astype(BF16)
    wp["head_params"] = (k_k, k_a, r_k.reshape(-1), ln_x_w, ln_x_b)
    return wp


def _run_group(x, mod3, pos, wp, prm, shift_prev, wkv_prev, conv_prev, tiles, attend):
    b, t, d = x.shape
    tb, tt = tiles["row"]
    cos, sin = _rope_tables(pos)

    h = _prenorm(x, mod3, 1, 0, prm["g_pre_mix"], tb, tt)
    tm = tiles["mm_rows"]
    q_lat = _matmul(h, wp["w_q"], tm, Q_LORA).reshape(b, t, Q_LORA)
    kvr = _matmul(h, wp["w_kvr"], tm, wp["w_kvr"].shape[1]).reshape(b, t, -1)
    rw = _matmul(h, wp["w_rw"], tm, 768).reshape(b, t, RW_PAD)

    qtb, qtt = tiles["q"]
    qcat = _qprep(q_lat, prm["g_q_latent"], wp["wq_all"], wp["wuk_t"], cos, sin, qtb, qtt,
                  tiles["cat_dtype"])
    kv_c, k_pe, kcat = _kvprep(kvr, prm["g_kv_latent"], cos, sin, tb, tt, tiles["cat_dtype"])
    o_lat = attend(qcat, kcat)
    otb, ott = tiles["o"]
    o_mla = _mla_out(o_lat, wp["wuv_t"], otb, ott)

    rtb, rtt = tiles["rw"]
    x5, g, new_shift = _rwprep(rw, shift_prev, wp["mu"], prm["w0"], prm["a0"],
                               wp["wd"], wp["wa"], wp["wg"], rtb, rtt)
    lay = _LaneLayout(b)
    o_scan, s_fin = _scan(lay.tokens_in(x5, t), lay.state_in(wkv_prev),
                          *[lay.param(p) for p in wp["head_params"]], tiles["scan_tc"])
    o_pre = lay.tokens_out(o_scan, t)
    new_wkv = lay.state_out(s_fin)

    x1 = _outproj(o_mla, o_pre, g.reshape(b * t, RWKV_WIDTH), wp["w_out"], x, mod3, 2,
                  prm["g_post_mix"], tb, tt)

    h2 = _prenorm(x1, mod3, 4, 3, prm["g_pre_ffn"], tb, tt)
    ftb, ftt = tiles["ffn"]
    act, conv_state = _ffn_in(h2, wp["w_ffn_in"], prm["conv_w"], prm["conv_b"], conv_prev,
                              b, t, ftb, ftt)
    f = _matmul(act, wp["w_ffn_out"], 512, 512)
    y = _final(x1, f, mod3, 5, prm["g_post_ffn"], tb, tt)
    return y, kv_c, k_pe, new_shift[:, 0, :RWKV_PROJ], new_wkv, conv_state


def kernel(x_prompt, x_sample, c_prompt, c_sample, cache_kv_latent, cache_k_rope, page_table,
           state_rwkv_shift, state_rwkv_wkv, state_ffn_conv, w_ada, b_ada, g_pre_mix, g_post_mix,
           g_pre_ffn, g_post_ffn, w_in, g_q_latent, w_uq, g_kv_latent, w_uk, w_uv, mu_shift, w0,
           w_decay_up, a0, w_iclr_up, w_gate_up, k_k, k_a, r_k, ln_x_w, ln_x_b, w_out, w_ffn_in,
           conv_w, conv_b, w_ffn_out):
    bp, tp, d = x_prompt.shape
    bs, ts, _ = x_sample.shape
    wp = _prep_weights(w_in, w_uq, w_uk, w_uv, mu_shift, w_decay_up, w_iclr_up, w_gate_up,
                       w_out, w_ffn_in, w_ffn_out, k_k, k_a, r_k, ln_x_w, ln_x_b)
    prm = dict(g_pre_mix=g_pre_mix, g_post_mix=g_post_mix, g_pre_ffn=g_pre_ffn,
               g_post_ffn=g_post_ffn, g_q_latent=g_q_latent, g_kv_latent=g_kv_latent,
               w0=w0, a0=a0, conv_w=conv_w, conv_b=conv_b)

    n_c = bp + bs
    n_c_pad = -(-n_c // 8) * 8
    c_all = jnp.concatenate([c_prompt, c_sample, jnp.zeros((n_c_pad - n_c, d), F32)], axis=0)
    mod = _adaln_mod(c_all, w_ada, b_ada)
    mod_p = mod[:bp].reshape(bp, 1, 6 * d)
    mod_s = mod[bp:n_c].reshape(bs, 1, 6 * d)

    tiles_p = dict(row=(1, 512), ffn=(1, 1024), mm_rows=1024, q=(1, 256), o=(1, 512),
                   rw=(1, 128), scan_tc=16, cat_dtype=BF16)
    outs_p = _run_group(
        x_prompt, mod_p, jnp.arange(tp), wp, prm,
        jnp.zeros((bp, 1, RW_PAD), F32),
        jnp.zeros((bp, RWKV_HEADS, RWKV_HEAD, RWKV_HEAD), F32),
        jnp.zeros((bp, CONV_W - 1, D_FF), F32),
        tiles_p, _attn_prompt)

    past_len = page_table.shape[1] * cache_kv_latent.shape[1]
    tiles_s = dict(row=(32, ts), ffn=(bs, ts), mm_rows=1024, q=(16, ts), o=(64, ts),
                   rw=(16, ts), scan_tc=ts, cat_dtype=F32)
    shift_s = jnp.pad(state_rwkv_shift, ((0, 0), (0, RW_PAD - RWKV_PROJ)))[:, None, :]
    cache_kr_t = jnp.swapaxes(cache_k_rope, 1, 2)
    attend_s = lambda qcat, kcat: _attn_sample(page_table, qcat, kcat, cache_kv_latent, cache_kr_t)
    outs_s = _run_group(
        x_sample, mod_s, past_len + jnp.arange(ts), wp, prm,
        shift_s, state_rwkv_wkv, state_ffn_conv, tiles_s, attend_s)

    y_p, kv_p, kr_p, sh_p, wkv_p, cs_p = outs_p
    y_s, kv_s, kr_s, sh_s, wkv_s, cs_s = outs_s
    return (y_p, y_s, kv_p, kr_p, sh_p, wkv_p, cs_p, kv_s, kr_s, sh_s, wkv_s, cs_s)
```

```python
import functools

import jax
import jax.numpy as jnp
from jax import lax
from jax.experimental import pallas as pl
from jax.experimental.pallas import tpu as pltpu

F32 = jnp.float32
BF16 = jnp.bfloat16

D_MODEL = 4096
MLA_HEADS = 16
QK_NOPE = 128
ROPE_DIM = 64
V_DIM = 128
Q_LORA = 896
KV_LORA = 512
ROPE_THETA = 10000.0
ATTN_SCALE = (QK_NOPE + ROPE_DIM) ** -0.5
RWKV_HEAD = 64
RWKV_WIDTH = D_MODEL - MLA_HEADS * V_DIM
RWKV_HEADS = RWKV_WIDTH // RWKV_HEAD
DECAY_LORA = 128
ICLR_LORA = 128
GATE_LORA = 480
RWKV_PROJ = 3 * RWKV_WIDTH + DECAY_LORA + ICLR_LORA + GATE_LORA
MLA_PROJ = Q_LORA + KV_LORA + ROPE_DIM
D_FF = 11008
CONV_W = 3
NORM_EPS = 1e-6
GN_EPS = 64e-5
PAGE_SIZE = 128

LANES = 128
RW_PAD = 6912
GATE_PAD = 512
QK_CAT = KV_LORA + LANES
VMEM_LIMIT = 56 * 1024 * 1024


def _params(sem):
    return pltpu.CompilerParams(dimension_semantics=sem, vmem_limit_bytes=VMEM_LIMIT)


def _rms(x, g):
    ms = jnp.mean(x * x, axis=-1, keepdims=True)
    return x * lax.rsqrt(ms + NORM_EPS) * g


def _mod_body(c_ref, w_ref, b_ref, o_ref):
    c = c_ref[...]
    s = c * jax.nn.sigmoid(c)
    o_ref[...] = jnp.dot(s.astype(BF16), w_ref[...].astype(BF16),
                         preferred_element_type=F32) + b_ref[...]


def _adaln_mod(c_all, w_ada, b_ada):
    m, d = c_all.shape
    n = w_ada.shape[1]
    tn = 512
    return pl.pallas_call(
        _mod_body,
        grid=(n // tn,),
        in_specs=[pl.BlockSpec((m, d), lambda j: (0, 0)),
                  pl.BlockSpec((d, tn), lambda j: (0, j)),
                  pl.BlockSpec((1, tn), lambda j: (0, j))],
        out_specs=pl.BlockSpec((m, tn), lambda j: (0, j)),
        out_shape=jax.ShapeDtypeStruct((m, n), F32),
        compiler_params=_params(("arbitrary",)),
        name="adaln_mod",
    )(c_all, w_ada, b_ada.reshape(1, n))


def _prenorm_body(x_ref, sc_ref, sh_ref, g_ref, o_ref):
    tb, tt, d = x_ref.shape
    y = _rms(x_ref[...], g_ref[...])
    h = y * (1.0 + sc_ref[...]) + sh_ref[...]
    o_ref[...] = h.reshape(tb * tt, d).astype(BF16)


def _prenorm(x, mod3, sc_idx, sh_idx, g, tb, tt):
    b, t, d = x.shape
    nt = t // tt
    return pl.pallas_call(
        _prenorm_body,
        grid=(b // tb, nt),
        in_specs=[pl.BlockSpec((tb, tt, d), lambda i, j: (i, j, 0)),
                  pl.BlockSpec((tb, 1, d), lambda i, j: (i, 0, sc_idx)),
                  pl.BlockSpec((tb, 1, d), lambda i, j: (i, 0, sh_idx)),
                  pl.BlockSpec((1, d), lambda i, j: (0, 0))],
        out_specs=pl.BlockSpec((tb * tt, d), lambda i, j: (i * nt + j, 0)),
        out_shape=jax.ShapeDtypeStruct((b * t, d), BF16),
        compiler_params=_params(("arbitrary", "arbitrary")),
        name="prenorm",
    )(x, mod3, mod3, g.reshape(1, d))


def _mm_body(a_ref, w_ref, o_ref):
    o_ref[...] = jnp.dot(a_ref[...], w_ref[...], preferred_element_type=F32).astype(o_ref.dtype)


def _matmul(a, w, tm, tn, out_dtype=F32):
    n, k = a.shape
    m = w.shape[1]
    return pl.pallas_call(
        _mm_body,
        grid=(n // tm, m // tn),
        in_specs=[pl.BlockSpec((tm, k), lambda i, j: (i, 0)),
                  pl.BlockSpec((k, tn), lambda i, j: (0, j))],
        out_specs=pl.BlockSpec((tm, tn), lambda i, j: (i, j)),
        out_shape=jax.ShapeDtypeStruct((n, m), out_dtype),
        compiler_params=_params(("arbitrary", "arbitrary")),
        name="matmul",
    )(a, w)


def _qprep_body(x_ref, g_ref, wq_ref, wuk_ref, cos_ref, sin_ref, o_ref):
    tb, tt, _ = x_ref.shape
    rows = tb * tt
    xn = _rms(x_ref[...], g_ref[...]).reshape(rows, Q_LORA).astype(BF16)
    q = jnp.dot(xn, wq_ref[...], preferred_element_type=F32)
    cos = cos_ref[...]
    sin = sin_ref[...]
    hw = MLA_HEADS * LANES
    for h in range(MLA_HEADS):
        qn = q[:, h * LANES:(h + 1) * LANES].astype(BF16)
        qa = jnp.dot(qn, wuk_ref[h], preferred_element_type=F32)
        o_ref[:, h, :, 0:KV_LORA] = qa.reshape(tb, tt, KV_LORA).astype(o_ref.dtype)
        pe = q[:, hw + h * LANES: hw + (h + 1) * LANES].reshape(tb, tt, LANES)
        pes = q[:, 2 * hw + h * LANES: 2 * hw + (h + 1) * LANES].reshape(tb, tt, LANES)
        o_ref[:, h, :, KV_LORA:QK_CAT] = (pe * cos + pes * sin).astype(o_ref.dtype)


def _qprep(q_lat, g_q, wq_all, wuk_t, cos, sin, tb, tt, out_dtype):
    b, t, _ = q_lat.shape
    return pl.pallas_call(
        _qprep_body,
        grid=(b // tb, t // tt),
        in_specs=[pl.BlockSpec((tb, tt, Q_LORA), lambda i, j: (i, j, 0)),
                  pl.BlockSpec((1, Q_LORA), lambda i, j: (0, 0)),
                  pl.BlockSpec(wq_all.shape, lambda i, j: (0, 0)),
                  pl.BlockSpec(wuk_t.shape, lambda i, j: (0, 0, 0)),
                  pl.BlockSpec((1, tt, LANES), lambda i, j: (0, j, 0)),
                  pl.BlockSpec((1, tt, LANES), lambda i, j: (0, j, 0))],
        out_specs=pl.BlockSpec((tb, MLA_HEADS, tt, QK_CAT), lambda i, j: (i, 0, j, 0)),
        out_shape=jax.ShapeDtypeStruct((b, MLA_HEADS, t, QK_CAT), out_dtype),
        compiler_params=_params(("arbitrary", "arbitrary")),
        name="q_prep",
    )(q_lat, g_q.reshape(1, Q_LORA), wq_all, wuk_t, cos, sin)


def _kvprep_body(x_ref, g_ref, cos_ref, sin_ref, kvc_ref, kpe_ref, kcat_ref):
    x = x_ref[...]
    kvc = _rms(x[..., 0:KV_LORA], g_ref[...])
    kpe = (x[..., KV_LORA:KV_LORA + LANES] * cos_ref[...]
           + x[..., KV_LORA + LANES:KV_LORA + 2 * LANES] * sin_ref[...])
    kvc_ref[...] = kvc
    kpe_ref[...] = kpe[..., 0:ROPE_DIM]
    kcat_ref[..., 0:KV_LORA] = kvc.astype(kcat_ref.dtype)
    kcat_ref[..., KV_LORA:QK_CAT] = kpe.astype(kcat_ref.dtype)


def _kvprep(kvr, g_kv, cos, sin, tb, tt, cat_dtype):
    b, t, w = kvr.shape
    blk = lambda n: pl.BlockSpec((tb, tt, n), lambda i, j: (i, j, 0))
    return pl.pallas_call(
        _kvprep_body,
        grid=(b // tb, t // tt),
        in_specs=[blk(w),
                  pl.BlockSpec((1, KV_LORA), lambda i, j: (0, 0)),
                  pl.BlockSpec((1, tt, LANES), lambda i, j: (0, j, 0)),
                  pl.BlockSpec((1, tt, LANES), lambda i, j: (0, j, 0))],
        out_specs=[blk(KV_LORA), blk(ROPE_DIM), blk(QK_CAT)],
        out_shape=[jax.ShapeDtypeStruct((b, t, KV_LORA), F32),
                   jax.ShapeDtypeStruct((b, t, ROPE_DIM), F32),
                   jax.ShapeDtypeStruct((b, t, QK_CAT), cat_dtype)],
        compiler_params=_params(("arbitrary", "arbitrary")),
        name="kv_prep",
    )(kvr, g_kv.reshape(1, KV_LORA), cos, sin)


ATT_TQ = 256
ATT_TK = 512
ATT_HG = 8
ATT_CHAINS = 8


def _attn_prompt_body(q_ref, k_ref, o_ref, m_ref, l_ref, acc_ref):
    qi = pl.program_id(1)
    rows = ATT_HG * ATT_TQ
    q = q_ref[0].reshape(rows, QK_CAT)
    m_ref[...] = jnp.full(m_ref.shape, -jnp.inf, F32)
    l_ref[...] = jnp.zeros(l_ref.shape, F32)
    acc_ref[...] = jnp.zeros(acc_ref.shape, F32)
    sub = rows // ATT_CHAINS
    qpos = qi * ATT_TQ + (lax.broadcasted_iota(jnp.int32, (sub, ATT_TK), 0) & (ATT_TQ - 1))
    col = lax.broadcasted_iota(jnp.int32, (sub, ATT_TK), 1)

    def body(kb, carry):
        kblk = k_ref[0, pl.ds(pl.multiple_of(kb * ATT_TK, ATT_TK), ATT_TK), :]
        vblk = kblk[:, 0:KV_LORA]
        visible = kb * ATT_TK + col <= qpos
        scores = [lax.dot_general(q[c * sub:(c + 1) * sub], kblk, (((1,), (1,)), ((), ())),
                                  preferred_element_type=F32) for c in range(ATT_CHAINS)]
        for c in range(ATT_CHAINS):
            rs = pl.ds(c * sub, sub)
            s = jnp.where(visible, scores[c] * ATTN_SCALE, -jnp.inf)
            m_prev = m_ref[rs, :]
            m_new = jnp.maximum(m_prev, jnp.max(s, axis=-1, keepdims=True))
            alpha = jnp.exp(m_prev - m_new)
            p = jnp.exp(s - m_new)
            l_ref[rs, :] = alpha * l_ref[rs, :] + jnp.sum(p, axis=-1, keepdims=True)
            acc_ref[rs, :] = alpha * acc_ref[rs, :] + jnp.dot(
                p.astype(BF16), vblk, preferred_element_type=F32)
            m_ref[rs, :] = m_new
        return carry

    n_kb = ((qi + 1) * ATT_TQ + ATT_TK - 1) // ATT_TK
    lax.fori_loop(0, n_kb, body, 0)
    o = acc_ref[...] / l_ref[...]
    o_ref[0] = o.reshape(ATT_HG, ATT_TQ, KV_LORA).astype(o_ref.dtype)


def _attn_prompt(qcat, kcat):
    b, h, t, _ = qcat.shape
    rows = ATT_HG * ATT_TQ
    return pl.pallas_call(
        _attn_prompt_body,
        grid=(b, t // ATT_TQ, h // ATT_HG),
        in_specs=[pl.BlockSpec((1, ATT_HG, ATT_TQ, QK_CAT), lambda i, j, g: (i, g, j, 0)),
                  pl.BlockSpec((1, t, QK_CAT), lambda i, j, g: (i, 0, 0))],
        out_specs=pl.BlockSpec((1, ATT_HG, ATT_TQ, KV_LORA), lambda i, j, g: (i, g, j, 0)),
        out_shape=jax.ShapeDtypeStruct((b, h, t, KV_LORA), BF16),
        scratch_shapes=[pltpu.VMEM((rows, 1), F32), pltpu.VMEM((rows, 1), F32),
                        pltpu.VMEM((rows, KV_LORA), F32)],
        compiler_params=_params(("arbitrary", "arbitrary", "arbitrary")),
        name="attn_prompt",
    )(qcat, kcat)


CHUNK_PAGES = 32
SUB_KEYS = 1024


def _attn_sample_body(pt_ref, q_ref, knew_ref, ckv_hbm, ckr_hbm, o_ref,
                      kc_buf, kp_buf, sem, m_ref, l_ref, acc_ref):
    b = pl.program_id(0)
    k = pl.program_id(1)
    nb = pl.num_programs(0)
    nk = pl.num_programs(1)
    ts = q_ref.shape[2]
    rows = MLA_HEADS * ts
    step = b * nk + k
    slot = step & 1

    def chunk_copies(seq, chunk, sl):
        cps = []
        for j in range(CHUNK_PAGES):
            page = pt_ref[seq, chunk * CHUNK_PAGES + j]
            dst = pl.ds(j * PAGE_SIZE, PAGE_SIZE)
            cps.append(pltpu.make_async_copy(ckv_hbm.at[page], kc_buf.at[sl, dst, :],
                                             sem.at[0, sl]))
            cps.append(pltpu.make_async_copy(ckr_hbm.at[page], kp_buf.at[sl, j], sem.at[1, sl]))
        return cps

    def start_all(cps):
        for n, cp in enumerate(cps):
            cp.start(priority=(n // 2) % 2)

    @pl.when(step == 0)
    def _():
        start_all(chunk_copies(0, 0, 0))

    @pl.when(step + 1 < nb * nk)
    def _():
        nxt = step + 1
        start_all(chunk_copies(nxt // nk, nxt % nk, 1 - slot))

    @pl.when(k == 0)
    def _():
        m_ref[...] = jnp.full(m_ref.shape, -jnp.inf, F32)
        l_ref[...] = jnp.zeros(l_ref.shape, F32)
        acc_ref[...] = jnp.zeros(acc_ref.shape, F32)

    q = q_ref[0].reshape(rows, QK_CAT).astype(BF16)
    qa = q[:, 0:KV_LORA]
    qp = q[:, KV_LORA:KV_LORA + ROPE_DIM]
    nt = (((1,), (1,)), ((), ()))

    def update(s_parts, val_parts):
        s = jnp.concatenate(s_parts, axis=1) if len(s_parts) > 1 else s_parts[0]
        m_prev = m_ref[...]
        m_new = jnp.maximum(m_prev, jnp.max(s, axis=-1, keepdims=True))
        alpha = jnp.exp(m_prev - m_new)
        p = jnp.exp(s - m_new)
        l_ref[...] = alpha * l_ref[...] + jnp.sum(p, axis=-1, keepdims=True)
        p = p.astype(BF16)
        pv = None
        col = 0
        for vals in val_parts:
            n = vals.shape[0]
            part = jnp.dot(p[:, col:col + n], vals, preferred_element_type=F32)
            pv = part if pv is None else pv + part
            col += n
        acc_ref[...] = alpha * acc_ref[...] + pv
        m_ref[...] = m_new

    for cp in chunk_copies(b, k, slot):
        cp.wait()

    s_parts = []
    kc_parts = []
    for i in range(CHUNK_PAGES * PAGE_SIZE // SUB_KEYS):
        keys = pl.ds(i * SUB_KEYS, SUB_KEYS)
        kc = kc_buf[slot, keys, :].astype(BF16)
        pages = range(i * SUB_KEYS // PAGE_SIZE, (i + 1) * SUB_KEYS // PAGE_SIZE)
        kp = jnp.concatenate([kp_buf[slot, j] for j in pages], axis=1).astype(BF16)
        s = (lax.dot_general(qa, kc, nt, preferred_element_type=F32)
             + jnp.dot(qp, kp, preferred_element_type=F32))
        s_parts.append(s * ATTN_SCALE)
        kc_parts.append(kc)
    half = len(s_parts) // 2
    update(s_parts[:half], kc_parts[:half])
    update(s_parts[half:], kc_parts[half:])

    @pl.when(k == nk - 1)
    def _():
        knew = knew_ref[0]
        knew = jnp.concatenate([knew, jnp.zeros_like(knew)], axis=0).astype(BF16)
        s = lax.dot_general(q, knew, nt, preferred_element_type=F32) * ATTN_SCALE
        tq = lax.broadcasted_iota(jnp.int32, s.shape, 0) & (ts - 1)
        tk = lax.broadcasted_iota(jnp.int32, s.shape, 1)
        s = jnp.where(tk <= tq, s, -jnp.inf)
        update([s], [knew[:, 0:KV_LORA]])
        o = acc_ref[...] / l_ref[...]
        o_ref[0] = o.reshape(MLA_HEADS, ts, KV_LORA).astype(o_ref.dtype)


def _attn_sample(page_table, qcat, kcat_new, cache_kv, cache_kr_t):
    b, h, ts, _ = qcat.shape
    n_pages = page_table.shape[1]
    rows = h * ts
    chunk_keys = CHUNK_PAGES * PAGE_SIZE
    grid_spec = pltpu.PrefetchScalarGridSpec(
        num_scalar_prefetch=1,
        grid=(b, n_pages // CHUNK_PAGES),
        in_specs=[pl.BlockSpec((1, h, ts, QK_CAT), lambda i, g, pt: (i, 0, 0, 0)),
                  pl.BlockSpec((1, ts, QK_CAT), lambda i, g, pt: (i, 0, 0)),
                  pl.BlockSpec(memory_space=pl.ANY),
                  pl.BlockSpec(memory_space=pl.ANY)],
        out_specs=pl.BlockSpec((1, h, ts, KV_LORA), lambda i, g, pt: (i, 0, 0, 0)),
        scratch_shapes=[pltpu.VMEM((2, chunk_keys, KV_LORA), F32),
                        pltpu.VMEM((2, CHUNK_PAGES, ROPE_DIM, PAGE_SIZE), F32),
                        pltpu.SemaphoreType.DMA((2, 2)),
                        pltpu.VMEM((rows, 1), F32), pltpu.VMEM((rows, 1), F32),
                        pltpu.VMEM((rows, KV_LORA), F32)],
    )
    return pl.pallas_call(
        _attn_sample_body,
        grid_spec=grid_spec,
        out_shape=jax.ShapeDtypeStruct((b, h, ts, KV_LORA), F32),
        compiler_params=_params(("arbitrary", "arbitrary")),
        name="attn_sample",
    )(page_table, qcat, kcat_new, cache_kv, cache_kr_t)


def _mla_out_body(o_ref, w_ref, out_ref):
    tb, nh, tt, c = o_ref.shape
    for h in range(nh):
        o = o_ref[:, h].reshape(tb * tt, c).astype(BF16)
        out_ref[:, h * V_DIM:(h + 1) * V_DIM] = jnp.dot(
            o, w_ref[h], preferred_element_type=F32).astype(out_ref.dtype)


def _mla_out(o_lat, wuv_t, tb, tt):
    b, h, t, c = o_lat.shape
    nt = t // tt
    return pl.pallas_call(
        _mla_out_body,
        grid=(b // tb, nt),
        in_specs=[pl.BlockSpec((tb, h, tt, c), lambda i, j: (i, 0, j, 0)),
                  pl.BlockSpec((h, c, V_DIM), lambda i, j: (0, 0, 0))],
        out_specs=pl.BlockSpec((tb * tt, h * V_DIM), lambda i, j: (i * nt + j, 0)),
        out_shape=jax.ShapeDtypeStruct((b * t, h * V_DIM), BF16),
        compiler_params=_params(("arbitrary", "arbitrary")),
        name="mla_out",
    )(o_lat, wuv_t)


def _rwprep_body(x_ref, st_ref, mu_ref, w0_ref, a0_ref, wd_ref, wa_ref, wg_ref,
                 o5_ref, g_ref, sh_ref, *carry):
    ti = pl.program_id(1)
    tb, tt, c = x_ref.shape
    rows = tb * tt
    x = x_ref[...]
    last = x[:, tt - 1:tt, :]
    if carry:
        first = jnp.where(ti == 0, st_ref[...], carry[0][...])
        carry[0][...] = last
    else:
        first = st_ref[...]
    tpos = lax.broadcasted_iota(jnp.int32, x.shape, 1)
    prev = jnp.where(tpos == 0, first, pltpu.roll(x, 1, axis=1))
    sh_ref[...] = last
    mixed = x + (prev - x) * mu_ref[...]
    w = RWKV_WIDTH
    o5_ref[0] = mixed[..., 0:w]
    o5_ref[1] = mixed[..., w:2 * w]
    o5_ref[2] = mixed[..., 2 * w:3 * w]
    wd = mixed[..., 3 * w:3 * w + DECAY_LORA].reshape(rows, DECAY_LORA)
    ad = mixed[..., 3 * w + DECAY_LORA:3 * w + DECAY_LORA + ICLR_LORA].reshape(rows, ICLR_LORA)
    gd = mixed[..., 3 * w + DECAY_LORA + ICLR_LORA:c].reshape(rows, GATE_PAD)
    z = w0_ref[...] + jnp.dot(jnp.tanh(wd).astype(BF16), wd_ref[...], preferred_element_type=F32)
    w_log = -jax.nn.softplus(-z) - 0.5
    o5_ref[3] = jnp.exp(-jnp.exp(w_log)).reshape(tb, tt, w)
    a = jax.nn.sigmoid(a0_ref[...] + jnp.dot(ad.astype(BF16), wa_ref[...],
                                             preferred_element_type=F32))
    o5_ref[4] = a.reshape(tb, tt, w)
    g = jnp.dot(jax.nn.sigmoid(gd).astype(BF16), wg_ref[...], preferred_element_type=F32)
    g_ref[...] = g.reshape(tb, tt, w)


def _rwprep(rw, shift_prev, mu_p, w0, a0, wd, wa, wg, tb, tt):
    b, t, c = rw.shape
    w = RWKV_WIDTH
    full = lambda a: pl.BlockSpec(a.shape, lambda i, j: (0,) * a.ndim)
    mu_p = mu_p.reshape(1, c)
    w0 = w0.reshape(1, w)
    a0 = a0.reshape(1, w)
    return pl.pallas_call(
        _rwprep_body,
        grid=(b // tb, t // tt),
        in_specs=[pl.BlockSpec((tb, tt, c), lambda i, j: (i, j, 0)),
                  pl.BlockSpec((tb, 1, c), lambda i, j: (i, 0, 0)),
                  full(mu_p), full(w0), full(a0), full(wd), full(wa), full(wg)],
        out_specs=[pl.BlockSpec((5, tb, tt, w), lambda i, j: (0, i, j, 0)),
                   pl.BlockSpec((tb, tt, w), lambda i, j: (i, j, 0)),
                   pl.BlockSpec((tb, 1, c), lambda i, j: (i, 0, 0))],
        out_shape=[jax.ShapeDtypeStruct((5, b, t, w), F32),
                   jax.ShapeDtypeStruct((b, t, w), F32),
                   jax.ShapeDtypeStruct((b, 1, c), F32)],
        scratch_shapes=[pltpu.VMEM((tb, 1, c), F32)] if t > tt else [],
        compiler_params=_params(("arbitrary", "arbitrary")),
        name="rwkv_prep",
    )(rw, shift_prev, mu_p, w0, a0, wd, wa, wg)


def _scan_body(x_ref, s0_ref, kkp_ref, kap_ref, rkp_ref, lnw_ref, lnb_ref,
               o_ref, sfin_ref, s_ref, kk_ref, b_ref, kh_ref):
    ti = pl.program_id(1)
    tc = x_ref.shape[2]
    n = RWKV_HEAD

    @pl.when(ti == 0)
    def _():
        for i in range(n):
            s_ref[i] = s0_ref[0, pl.ds(i, n, stride=n), :]

    r = x_ref[0, 0]
    k = x_ref[1, 0]
    v = x_ref[2, 0]
    a = x_ref[4, 0]
    kk = k * kkp_ref[0]
    kk = kk / jnp.maximum(jnp.sqrt(jnp.sum(kk * kk, axis=1, keepdims=True)), 1e-12)
    kh = k * (1.0 + (a - 1.0) * kap_ref[0])
    kk_ref[...] = kk
    b_ref[...] = kk * a
    kh_ref[...] = kh

    def row(ref, t, i):
        return ref[t, pl.ds(i, 1), :]

    def step(t, s_kk):
        sa = -s_kk
        vt = x_ref[2, 0, t]
        t_next = jnp.minimum(t + 1, tc - 1)
        y = jnp.zeros((n, LANES), F32)
        s_kk_next = jnp.zeros((n, LANES), F32)
        for i in range(n):
            s_new = (s_ref[i] * x_ref[3, 0, t, pl.ds(i, 1), :]
                     + sa * row(b_ref, t, i)
                     + vt * row(kh_ref, t, i))
            s_ref[i] = s_new
            y = y + s_new * x_ref[0, 0, t, pl.ds(i, 1), :]
            s_kk_next = s_kk_next + s_new * row(kk_ref, t_next, i)
        o_ref[0, t] = y
        return s_kk_next

    s_kk0 = jnp.zeros((n, LANES), F32)
    for i in range(n):
        s_kk0 = s_kk0 + s_ref[i] * row(kk_ref, 0, i)
    lax.fori_loop(0, tc, step, s_kk0)

    y = o_ref[0]
    mu = jnp.mean(y, axis=1, keepdims=True)
    var = jnp.mean(jnp.square(y - mu), axis=1, keepdims=True)
    yn = (y - mu) * lax.rsqrt(var + GN_EPS) * lnw_ref[0] + lnb_ref[0]
    bonus = jnp.sum(r * kh * rkp_ref[0], axis=1, keepdims=True) * v
    o_ref[0] = yn + bonus

    @pl.when(ti == pl.num_programs(1) - 1)
    def _():
        for i in range(n):
            sfin_ref[0, pl.ds(i, n, stride=n), :] = s_ref[i]


def _scan(x5, s0, kkp, kap, rkp, lnw, lnb, tc):
    _, g, t, n, l = x5.shape
    par = pl.BlockSpec((1, n, l), lambda i, j: (i, 0, 0))
    return pl.pallas_call(
        _scan_body,
        grid=(g, t // tc),
        in_specs=[pl.BlockSpec((5, 1, tc, n, l), lambda i, j: (0, i, j, 0, 0)),
                  pl.BlockSpec((1, n * n, l), lambda i, j: (i, 0, 0)),
                  par, par, par, par, par],
        out_specs=[pl.BlockSpec((1, tc, n, l), lambda i, j: (i, j, 0, 0)),
                   pl.BlockSpec((1, n * n, l), lambda i, j: (i, 0, 0))],
        out_shape=[jax.ShapeDtypeStruct((g, t, n, l), F32),
                   jax.ShapeDtypeStruct((g, n * n, l), F32)],
        scratch_shapes=[pltpu.VMEM((n, n, l), F32), pltpu.VMEM((tc, n, l), F32),
                        pltpu.VMEM((tc, n, l), F32), pltpu.VMEM((tc, n, l), F32)],
        compiler_params=_params(("arbitrary", "arbitrary")),
        name="rwkv_scan",
    )(x5, s0, kkp, kap, rkp, lnw, lnb)


OUT_TK = 512


def _outproj_body(am_ref, op_ref, g_ref, w_ref, x_ref, gt_ref, gp_ref, o_ref):
    k = pl.program_id(2)
    nk = pl.num_programs(2)
    tb, tt, d = x_ref.shape

    @pl.when(k == 0)
    def _():
        o_ref[...] = jnp.zeros(o_ref.shape, F32)

    @pl.when(k < nk // 2)
    def _():
        o_ref[...] += jnp.dot(am_ref[...], w_ref[...],
                              preferred_element_type=F32).reshape(tb, tt, d)

    @pl.when(k >= nk // 2)
    def _():
        a = (op_ref[...] * g_ref[...]).astype(BF16)
        o_ref[...] += jnp.dot(a, w_ref[...], preferred_element_type=F32).reshape(tb, tt, d)

    @pl.when(k == nk - 1)
    def _():
        o_ref[...] = x_ref[...] + gt_ref[...] * _rms(o_ref[...], gp_ref[...])


def _outproj(o_mla, o_pre, g, w_out, x, mod3, gt_idx, g_post, tb, tt):
    b, t, d = x.shape
    nt = t // tt
    rows = tb * tt
    nk = d // OUT_TK
    half = nk // 2
    return pl.pallas_call(
        _outproj_body,
        grid=(b // tb, nt, nk),
        in_specs=[pl.BlockSpec((rows, OUT_TK), lambda i, j, k: (i * nt + j, jnp.minimum(k, half - 1))),
                  pl.BlockSpec((rows, OUT_TK), lambda i, j, k: (i * nt + j, jnp.maximum(k - half, 0))),
                  pl.BlockSpec((rows, OUT_TK), lambda i, j, k: (i * nt + j, jnp.maximum(k - half, 0))),
                  pl.BlockSpec((OUT_TK, d), lambda i, j, k: (k, 0)),
                  pl.BlockSpec((tb, tt, d), lambda i, j, k: (i, j, 0)),
                  pl.BlockSpec((tb, 1, d), lambda i, j, k: (i, 0, gt_idx)),
                  pl.BlockSpec((1, d), lambda i, j, k: (0, 0))],
        out_specs=pl.BlockSpec((tb, tt, d), lambda i, j, k: (i, j, 0)),
        out_shape=jax.ShapeDtypeStruct((b, t, d), F32),
        compiler_params=_params(("arbitrary", "arbitrary", "arbitrary")),
        name="out_proj",
    )(o_mla, o_pre, g, w_out, x, mod3, g_post.reshape(1, d))


FFN_TF = 256


def _ffn_in_body(h_ref, wg_ref, wu_ref, cw_ref, cb_ref, st_ref, a_ref, cs_ref, wg_bf, wu_bf,
                 *carry, tb, tt):
    ti = pl.program_id(2)
    tf = wg_ref.shape[1]

    @pl.when((pl.program_id(1) == 0) & (ti == 0))
    def _():
        wg_bf[...] = wg_ref[...].astype(BF16)
        wu_bf[...] = wu_ref[...].astype(BF16)

    h = h_ref[...]
    gate = jnp.dot(h, wg_bf[...], preferred_element_type=F32).reshape(tb, tt, tf)
    up = jnp.dot(h, wu_bf[...], preferred_element_type=F32).reshape(tb, tt, tf)
    last2 = gate[:, tt - 2:tt, :]
    if carry:
        prev2 = jnp.where(ti == 0, st_ref[...], carry[0][...])
        carry[0][...] = last2
    else:
        prev2 = st_ref[...]
    p0 = prev2[:, 0:1, :]
    p1 = prev2[:, 1:2, :]
    tpos = lax.broadcasted_iota(jnp.int32, gate.shape, 1)
    g1 = jnp.where(tpos == 0, p1, pltpu.roll(gate, 1, axis=1))
    g2 = jnp.where(tpos == 0, p0, jnp.where(tpos == 1, p1, pltpu.roll(gate, 2, axis=1)))
    cs_ref[...] = last2
    cw = cw_ref[...]
    gate_c = cb_ref[...] + cw[0:1, :] * g2
    gate_c = gate_c + cw[1:2, :] * g1
    gate_c = gate_c + cw[2:3, :] * gate
    a_ref[...] = (jax.nn.gelu(gate_c) * up).reshape(tb * tt, tf).astype(BF16)


def _ffn_in(h2, w_ffn_in, conv_w, conv_b, conv_prev, b, t, tb, tt):
    d = h2.shape[1]
    nt = t // tt
    rows = tb * tt
    tf = FFN_TF
    nj = D_FF // tf
    return pl.pallas_call(
        functools.partial(_ffn_in_body, tb=tb, tt=tt),
        grid=(nj, b // tb, nt),
        in_specs=[pl.BlockSpec((rows, d), lambda j, i, k: (i * nt + k, 0)),
                  pl.BlockSpec((d, tf), lambda j, i, k: (0, j)),
                  pl.BlockSpec((d, tf), lambda j, i, k: (0, nj + j)),
                  pl.BlockSpec((CONV_W, tf), lambda j, i, k: (0, j)),
                  pl.BlockSpec((1, tf), lambda j, i, k: (0, j)),
                  pl.BlockSpec((tb, CONV_W - 1, tf), lambda j, i, k: (i, 0, j))],
        out_specs=[pl.BlockSpec((rows, tf), lambda j, i, k: (i * nt + k, j)),
                   pl.BlockSpec((tb, CONV_W - 1, tf), lambda j, i, k: (i, 0, j))],
        out_shape=[jax.ShapeDtypeStruct((b * t, D_FF), BF16),
                   jax.ShapeDtypeStruct((b, CONV_W - 1, D_FF), F32)],
        scratch_shapes=([pltpu.VMEM((d, tf), BF16), pltpu.VMEM((d, tf), BF16)]
                        + ([pltpu.VMEM((tb, CONV_W - 1, tf), F32)] if nt > 1 else [])),
        compiler_params=_params(("arbitrary", "arbitrary", "arbitrary")),
        name="conv_ffn_in",
    )(h2, w_ffn_in, w_ffn_in, conv_w, conv_b.reshape(1, D_FF), conv_prev)


def _final_body(x_ref, f_ref, gt_ref, gp_ref, o_ref):
    tb, tt, d = x_ref.shape
    f = f_ref[...].reshape(tb, tt, d)
    o_ref[...] = x_ref[...] + gt_ref[...] * _rms(f, gp_ref[...])


def _final(x, f, mod3, gt_idx, g_post, tb, tt):
    b, t, d = x.shape
    nt = t // tt
    return pl.pallas_call(
        _final_body,
        grid=(b // tb, nt),
        in_specs=[pl.BlockSpec((tb, tt, d), lambda i, j: (i, j, 0)),
                  pl.BlockSpec((tb * tt, d), lambda i, j: (i * nt + j, 0)),
                  pl.BlockSpec((tb, 1, d), lambda i, j: (i, 0, gt_idx)),
                  pl.BlockSpec((1, d), lambda i, j: (0, 0))],
        out_specs=pl.BlockSpec((tb, tt, d), lambda i, j: (i, j, 0)),
        out_shape=jax.ShapeDtypeStruct((b, t, d), F32),
        compiler_params=_params(("arbitrary", "arbitrary")),
        name="residual_norm",
    )(x, f, mod3, g_post.reshape(1, d))


def _rope_tables(pos):
    half = ROPE_DIM // 2
    inv = ROPE_THETA ** (-jnp.arange(half, dtype=F32) * 2.0 / ROPE_DIM)
    ang = pos.astype(F32)[:, None] * inv[None, :]
    cos, sin = jnp.cos(ang), jnp.sin(ang)
    z = jnp.zeros((pos.shape[0], LANES - ROPE_DIM), F32)
    cos_t = jnp.concatenate([cos, cos, z], axis=-1)[None]
    sin_t = jnp.concatenate([-sin, sin, z], axis=-1)[None]
    return cos_t, sin_t


def _swap_halves(w):
    half = ROPE_DIM // 2
    return jnp.concatenate([w[..., half:], w[..., :half]], axis=-1)


class _LaneLayout:
    def __init__(self, b):
        n, hh = RWKV_HEAD, RWKV_HEADS
        self.b = b
        self.by_head = b == LANES
        assert self.by_head or b * hh == LANES, b
        self.groups = hh if self.by_head else 1

    def param(self, p):
        ph = p.reshape(RWKV_HEADS, RWKV_HEAD)
        if self.by_head:
            return jnp.broadcast_to(ph[:, :, None], (RWKV_HEADS, RWKV_HEAD, LANES))
        return jnp.tile(ph.T, (1, self.b))[None]

    def tokens_in(self, x5, t):
        x = x5.reshape(5, self.b, t, RWKV_HEADS, RWKV_HEAD)
        if self.by_head:
            return x.transpose(0, 3, 2, 4, 1)
        return x.transpose(0, 2, 4, 1, 3).reshape(5, 1, t, RWKV_HEAD, LANES)

    def tokens_out(self, o, t):
        if self.by_head:
            x = o.transpose(3, 1, 0, 2)
        else:
            x = o.reshape(t, RWKV_HEAD, self.b, RWKV_HEADS).transpose(2, 0, 3, 1)
        return x.reshape(self.b * t, RWKV_WIDTH)

    def state_in(self, s):
        n = RWKV_HEAD
        if self.by_head:
            return s.transpose(1, 2, 3, 0).reshape(RWKV_HEADS, n * n, LANES)
        return s.transpose(2, 3, 0, 1).reshape(1, n * n, LANES)

    def state_out(self, s):
        n = RWKV_HEAD
        if self.by_head:
            return s.reshape(RWKV_HEADS, n, n, LANES).transpose(3, 0, 1, 2)
        return s.reshape(n, n, self.b, RWKV_HEADS).transpose(2, 3, 0, 1)


def _prep_weights(w_in, w_uq, w_uk, w_uv, mu_shift, w_decay_up, w_iclr_up, w_gate_up,
                  w_out, w_ffn_in, w_ffn_out, k_k, k_a, r_k, ln_x_w, ln_x_b):
    d = D_MODEL
    z64 = jnp.zeros((d, LANES - ROPE_DIM), F32)
    w_kr = w_in[:, Q_LORA + KV_LORA:MLA_PROJ]
    wp = {}
    wp["w_q"] = w_in[:, :Q_LORA].astype(BF16)
    wp["w_kvr"] = jnp.concatenate(
        [w_in[:, Q_LORA:Q_LORA + KV_LORA], w_kr, z64, _swap_halves(w_kr), z64], axis=1).astype(BF16)
    wp["w_rw"] = jnp.pad(w_in[:, MLA_PROJ:], ((0, 0), (0, RW_PAD - RWKV_PROJ))).astype(BF16)
    wq3 = w_uq.reshape(Q_LORA, MLA_HEADS, QK_NOPE + ROPE_DIM)
    pe = wq3[:, :, QK_NOPE:]
    padl = ((0, 0), (0, 0), (0, LANES - ROPE_DIM))
    wp["wq_all"] = jnp.concatenate(
        [wq3[:, :, :QK_NOPE].reshape(Q_LORA, -1),
         jnp.pad(pe, padl).reshape(Q_LORA, -1),
         jnp.pad(_swap_halves(pe), padl).reshape(Q_LORA, -1)], axis=1).astype(BF16)
    wp["wuk_t"] = w_uk.transpose(1, 2, 0).astype(BF16)
    wp["wuv_t"] = w_uv.transpose(1, 0, 2).astype(BF16)
    wp["mu"] = jnp.pad(mu_shift, (0, RW_PAD - RWKV_PROJ))
    wp["wd"] = w_decay_up.astype(BF16)
    wp["wa"] = w_iclr_up.astype(BF16)
    wp["wg"] = jnp.pad(w_gate_up, ((0, GATE_PAD - GATE_LORA), (0, 0))).astype(BF16)
    wp["w_out"] = w_out.astype(BF16)
    wp["w_ffn_in"] = w_ffn_in
    wp["w_ffn_out"] = w_ffn_out.astype(BF16)
    wp["head_params"] = (k_k, k_a, r_k.reshape(-1), ln_x_w, ln_x_b)
    return wp


def _run_group(x, mod3, pos, wp, prm, shift_prev, wkv_prev, conv_prev, tiles, attend):
    b, t, d = x.shape
    tb, tt = tiles["row"]
    cos, sin = _rope_tables(pos)

    h = _prenorm(x, mod3, 1, 0, prm["g_pre_mix"], tb, tt)
    tm = tiles["mm_rows"]
    q_lat = _matmul(h, wp["w_q"], tm, Q_LORA).reshape(b, t, Q_LORA)
    kvr = _matmul(h, wp["w_kvr"], tm, wp["w_kvr"].shape[1]).reshape(b, t, -1)
    rw = _matmul(h, wp["w_rw"], tm, 768).reshape(b, t, RW_PAD)

    qtb, qtt = tiles["q"]
    qcat = _qprep(q_lat, prm["g_q_latent"], wp["wq_all"], wp["wuk_t"], cos, sin, qtb, qtt,
                  tiles["cat_dtype"])
    kv_c, k_pe, kcat = _kvprep(kvr, prm["g_kv_latent"], cos, sin, tb, tt, tiles["cat_dtype"])
    o_lat = attend(qcat, kcat)
    otb, ott = tiles["o"]
    o_mla = _mla_out(o_lat, wp["wuv_t"], otb, ott)

    rtb, rtt = tiles["rw"]
    x5, g, new_shift = _rwprep(rw, shift_prev, wp["mu"], prm["w0"], prm["a0"],
                               wp["wd"], wp["wa"], wp["wg"], rtb, rtt)
    lay = _LaneLayout(b)
    o_scan, s_fin = _scan(lay.tokens_in(x5, t), lay.state_in(wkv_prev),
                          *[lay.param(p) for p in wp["head_params"]], tiles["scan_tc"])
    o_pre = lay.tokens_out(o_scan, t)
    new_wkv = lay.state_out(s_fin)

    x1 = _outproj(o_mla, o_pre, g.reshape(b * t, RWKV_WIDTH), wp["w_out"], x, mod3, 2,
                  prm["g_post_mix"], tb, tt)

    h2 = _prenorm(x1, mod3, 4, 3, prm["g_pre_ffn"], tb, tt)
    ftb, ftt = tiles["ffn"]
    act, conv_state = _ffn_in(h2, wp["w_ffn_in"], prm["conv_w"], prm["conv_b"], conv_prev,
                              b, t, ftb, ftt)
    f = _matmul(act, wp["w_ffn_out"], 512, 512)
    y = _final(x1, f, mod3, 5, prm["g_post_ffn"], tb, tt)
    return y, kv_c, k_pe, new_shift[:, 0, :RWKV_PROJ], new_wkv, conv_state


def kernel(x_prompt, x_sample, c_prompt, c_sample, cache_kv_latent, cache_k_rope, page_table,
           state_rwkv_shift, state_rwkv_wkv, state_ffn_conv, w_ada, b_ada, g_pre_mix, g_post_mix,
           g_pre_ffn, g_post_ffn, w_in, g_q_latent, w_uq, g_kv_latent, w_uk, w_uv, mu_shift, w0,
           w_decay_up, a0, w_iclr_up, w_gate_up, k_k, k_a, r_k, ln_x_w, ln_x_b, w_out, w_ffn_in,
           conv_w, conv_b, w_ffn_out):
    bp, tp, d = x_prompt.shape
    bs, ts, _ = x_sample.shape
    wp = _prep_weights(w_in, w_uq, w_uk, w_uv, mu_shift, w_decay_up, w_iclr_up, w_gate_up,
                       w_out, w_ffn_in, w_ffn_out, k_k, k_a, r_k, ln_x_w, ln_x_b)
    prm = dict(g_pre_mix=g_pre_mix, g_post_mix=g_post_mix, g_pre_ffn=g_pre_ffn,
               g_post_ffn=g_post_ffn, g_q_latent=g_q_latent, g_kv_latent=g_kv_latent,
               w0=w0, a0=a0, conv_w=conv_w, conv_b=conv_b)

    n_c = bp + bs
    n_c_pad = -(-n_c // 8) * 8
    c_all = jnp.concatenate([c_prompt, c_sample, jnp.zeros((n_c_pad - n_c, d), F32)], axis=0)
    mod = _adaln_mod(c_all, w_ada, b_ada)
    mod_p = mod[:bp].reshape(bp, 1, 6 * d)
    mod_s = mod[bp:n_c].reshape(bs, 1, 6 * d)

    tiles_p = dict(row=(1, 512), ffn=(1, 1024), mm_rows=1024, q=(1, 256), o=(1, 512),
                   rw=(1, 128), scan_tc=32, cat_dtype=BF16)
    outs_p = _run_group(
        x_prompt, mod_p, jnp.arange(tp), wp, prm,
        jnp.zeros((bp, 1, RW_PAD), F32),
        jnp.zeros((bp, RWKV_HEADS, RWKV_HEAD, RWKV_HEAD), F32),
        jnp.zeros((bp, CONV_W - 1, D_FF), F32),
        tiles_p, _attn_prompt)

    past_len = page_table.shape[1] * cache_kv_latent.shape[1]
    tiles_s = dict(row=(32, ts), ffn=(bs, ts), mm_rows=1024, q=(16, ts), o=(64, ts),
                   rw=(16, ts), scan_tc=ts, cat_dtype=F32)
    shift_s = jnp.pad(state_rwkv_shift, ((0, 0), (0, RW_PAD - RWKV_PROJ)))[:, None, :]
    cache_kr_t = jnp.swapaxes(cache_k_rope, 1, 2)
    attend_s = lambda qcat, kcat: _attn_sample(page_table, qcat, kcat, cache_kv_latent, cache_kr_t)
    outs_s = _run_group(
        x_sample, mod_s, past_len + jnp.arange(ts), wp, prm,
        shift_s, state_rwkv_wkv, state_ffn_conv, tiles_s, attend_s)

    y_p, kv_p, kr_p, sh_p, wkv_p, cs_p = outs_p
    y_s, kv_s, kr_s, sh_s, wkv_s, cs_s = outs_s
    return (y_p, y_s, kv_p, kr_p, sh_p, wkv_p, cs_p, kv_s, kr_s, sh_s, wkv_s, cs_s)
```
